```python
import math
import jax, jax.numpy as jnp
from jax import lax
import numpy as np

D_MODEL = 4096
BATCH = 4
SEQ = 2048
DEPTH = 1

HEAD_DIM = 128
ATTN_HEADS = D_MODEL // 2 // HEAD_DIM
ATTN_KV_HEADS = 2
ATTN_GROUP = ATTN_HEADS // ATTN_KV_HEADS
WINDOW = 128
BLOCK = 128
N_BUCKETS = 32
MAX_DISTANCE = 128
GLA_HEADS = 4
GLA_DV = D_MODEL // 2 // GLA_HEADS
GLA_DK = GLA_DV // 2
GLA_RANK = 16
GLA_TAU = 16.0
GLA_CHUNK = 64
ATTN_Q = ATTN_HEADS * HEAD_DIM
ATTN_KV = ATTN_KV_HEADS * HEAD_DIM
GLA_K = GLA_HEADS * GLA_DK
GLA_V = GLA_HEADS * GLA_DV
MIX_WIDTH = ATTN_Q + GLA_V
IN_WIDTH = ATTN_Q + 2 * ATTN_KV + 2 * GLA_K + GLA_V + GLA_RANK + GLA_V
PEER_HEADS = 8
N_KEYS = 128
N_EXPERTS = N_KEYS * N_KEYS
PEER_TOPK = 16
PEER_DKEY = 256
PEER_TOKEN_BLOCK = 128
N_MOD = 6
EPS = 1e-6
NEG_INF = -1e30

kernel_name = "hybrid_swa_gla_peer_adaln"


def rmsnorm(x, g):
    xf = x.astype(jnp.float32)
    y = xf * lax.rsqrt(jnp.mean(xf * xf, axis=-1, keepdims=True) + EPS)
    return (y * g.astype(jnp.float32)).astype(x.dtype)


def t5_causal_bucket(dist):
    max_exact = N_BUCKETS // 2
    d = jnp.maximum(dist, 0)
    log_ratio = jnp.log(jnp.maximum(d, 1).astype(jnp.float32) / max_exact) / math.log(MAX_DISTANCE / max_exact)
    large = max_exact + (log_ratio * (N_BUCKETS - max_exact)).astype(jnp.int32)
    large = jnp.minimum(large, N_BUCKETS - 1)
    return jnp.where(d < max_exact, d, large)


def sliding_window_attention(q, k, v, sinks, rel_bias):
    B, T = q.shape[0], q.shape[1]
    nb = T // BLOCK
    qb = q.reshape(B, nb, BLOCK, ATTN_KV_HEADS, ATTN_GROUP, HEAD_DIM)

    def with_prev(a):
        a = a.reshape(B, nb, BLOCK, ATTN_KV_HEADS, HEAD_DIM)
        prev = jnp.pad(a, ((0, 0), (1, 0), (0, 0), (0, 0), (0, 0)))[:, :-1]
        return jnp.concatenate([prev, a], axis=2)

    kb, vb = with_prev(k), with_prev(v)
    s = jnp.einsum('bnqhgd,bnkhd->bnhgqk', qb, kb,
                   preferred_element_type=jnp.float32) * (HEAD_DIM ** -0.5)
    qi = jnp.arange(BLOCK)[:, None]
    kj = jnp.arange(2 * BLOCK)[None, :]
    dist = qi + BLOCK - kj
    in_window = (dist >= 0) & (dist < WINDOW)
    bias = rel_bias.astype(jnp.float32)[t5_causal_bucket(dist)]
    bias = bias.transpose(2, 0, 1).reshape(ATTN_KV_HEADS, ATTN_GROUP, BLOCK, 2 * BLOCK)
    kpos = jnp.arange(nb)[:, None] * BLOCK - BLOCK + jnp.arange(2 * BLOCK)[None, :]
    mask = in_window[None] & (kpos >= 0)[:, None, :]
    s = jnp.where(mask[None, :, None, None], s + bias[None, None], NEG_INF)
    sink = sinks.astype(jnp.float32).reshape(ATTN_KV_HEADS, ATTN_GROUP)[None, None, :, :, None, None]
    m = jnp.maximum(jnp.max(s, axis=-1, keepdims=True), sink)
    p = jnp.exp(s - m)
    p = p / (jnp.sum(p, axis=-1, keepdims=True) + jnp.exp(sink - m))
    o = jnp.einsum('bnhgqk,bnkhd->bnqhgd', p.astype(v.dtype), vb)
    return o.reshape(B, T, ATTN_Q)


def gated_linear_attention(q, k, v, log_a):
    B, T, H, dk = q.shape
    dv = v.shape[-1]
    nc, C = T // GLA_CHUNK, GLA_CHUNK

    def chunked(a):
        return a.astype(jnp.float32).reshape(B, nc, C, H, a.shape[-1]).transpose(0, 1, 3, 2, 4)

    q, k, v, log_a = chunked(q) * (dk ** -0.5), chunked(k), chunked(v), chunked(log_a)
    b = jnp.cumsum(log_a, axis=3)
    b_last = b[:, :, :, -1:, :]
    q_dec = q * jnp.exp(b)
    k_inv = k * jnp.exp(-b)
    k_dec = k * jnp.exp(b_last - b)
    causal = jnp.tril(jnp.ones((C, C), dtype=bool))
    attn = jnp.where(causal, jnp.einsum('bnhid,bnhjd->bnhij', q_dec, k_inv), 0.0)
    o_intra = jnp.einsum('bnhij,bnhjv->bnhiv', attn, v)

    def step(S, inp):
        qd, kd, vc, dl = inp
        o = jnp.einsum('bhcd,bhdv->bhcv', qd, S)
        S = dl[..., None] * S + jnp.einsum('bhcd,bhcv->bhdv', kd, vc)
        return S, o

    S0 = jnp.zeros((B, H, dk, dv), jnp.float32)
    xs = (q_dec.swapaxes(0, 1), k_dec.swapaxes(0, 1), v.swapaxes(0, 1),
          jnp.exp(b_last[:, :, :, 0, :]).swapaxes(0, 1))
    _, o_inter = lax.scan(step, S0, xs)
    o = o_intra + o_inter.swapaxes(0, 1)
    return o.transpose(0, 1, 3, 2, 4).reshape(B, T, H, dv)


def peer(h, w_q, sub_keys, u, v):
    B, T, D = h.shape
    tokens = h.reshape(B * T, D)
    q = (tokens @ w_q).astype(jnp.float32).reshape(B * T, PEER_HEADS, 2, PEER_DKEY // 2)
    s = jnp.einsum('nhpd,hpkd->nhpk', q, sub_keys.astype(jnp.float32))
    s1, i1 = lax.top_k(s[:, :, 0], PEER_TOPK)
    s2, i2 = lax.top_k(s[:, :, 1], PEER_TOPK)
    cand_s = (s1[..., :, None] + s2[..., None, :]).reshape(B * T, PEER_HEADS, PEER_TOPK * PEER_TOPK)
    cand_i = (i1[..., :, None] * N_KEYS + i2[..., None, :]).reshape(B * T, PEER_HEADS, PEER_TOPK * PEER_TOPK)
    top_s, pos = lax.top_k(cand_s, PEER_TOPK)
    idx = jnp.take_along_axis(cand_i, pos, axis=-1)
    gates = jax.nn.softmax(top_s, axis=-1)
    nblk = (B * T) // PEER_TOKEN_BLOCK

    def block(args):
        xt, ib, gb = args
        act = jnp.einsum('phkd,pd->phk', u[ib], xt, preferred_element_type=jnp.float32)
        w = (gb * jax.nn.gelu(act, approximate=False)).astype(xt.dtype)
        return jnp.einsum('phk,phkd->pd', w, v[ib])

    out = lax.map(block, (tokens.reshape(nblk, PEER_TOKEN_BLOCK, D),
                          idx.reshape(nblk, PEER_TOKEN_BLOCK, PEER_HEADS, PEER_TOPK),
                          gates.reshape(nblk, PEER_TOKEN_BLOCK, PEER_HEADS, PEER_TOPK)))
    return out.reshape(B, T, D)


def hybrid_layer(x, mod, norm1_g, norm2_g, w_in, sinks, rel_bias, gk2_w, gk2_b,
                 gla_norm_g, w_out, peer_w_q, peer_keys, peer_u, peer_v):
    B, T, _ = x.shape
    shift1, scale1, gate1, shift2, scale2, gate2 = jnp.split(mod[:, None, :], N_MOD, axis=-1)
    h = rmsnorm(x, norm1_g) * (1 + scale1) + shift1
    proj = h @ w_in
    widths = [ATTN_Q, ATTN_KV, ATTN_KV, GLA_K, GLA_K, GLA_V, GLA_RANK, GLA_V]
    offs = [int(o) for o in np.cumsum(widths)[:-1]]
    aq, ak, av, gq, gk, gv, glow, gout = jnp.split(proj, offs, axis=-1)
    attn_out = sliding_window_attention(
        aq.reshape(B, T, ATTN_HEADS, HEAD_DIM), ak.reshape(B, T, ATTN_KV_HEADS, HEAD_DIM),
        av.reshape(B, T, ATTN_KV_HEADS, HEAD_DIM), sinks, rel_bias)
    log_a = jax.nn.log_sigmoid((glow @ gk2_w + gk2_b).astype(jnp.float32)) / GLA_TAU
    gla_o = gated_linear_attention(
        gq.reshape(B, T, GLA_HEADS, GLA_DK), gk.reshape(B, T, GLA_HEADS, GLA_DK),
        gv.reshape(B, T, GLA_HEADS, GLA_DV), log_a.reshape(B, T, GLA_HEADS, GLA_DK))
    gla_o = rmsnorm(gla_o, gla_norm_g) * jax.nn.silu(gout.reshape(B, T, GLA_HEADS, GLA_DV).astype(jnp.float32))
    mixed = jnp.concatenate([attn_out, gla_o.reshape(B, T, GLA_V).astype(x.dtype)], axis=-1) @ w_out
    x = x + gate1 * mixed
    h = rmsnorm(x, norm2_g) * (1 + scale2) + shift2
    return x + gate2 * peer(h, peer_w_q, peer_keys, peer_u, peer_v)


def setup_inputs(seed: int = 0) -> dict:
    key = jax.random.key(seed)
    ks = jax.random.split(key, 20)
    f32 = jnp.float32
    D = D_MODEL

    def nrm(k, shape, scale):
        return jax.random.normal(k, shape, f32) * scale

    return {
        "x": nrm(ks[0], (BATCH, SEQ, D), 1.0),
        "c": nrm(ks[1], (BATCH, D), 1.0),
        "w_ada": nrm(ks[2], (DEPTH, D, N_MOD * D), 0.5 * D ** -0.5),
        "b_ada": nrm(ks[3], (DEPTH, N_MOD * D), 0.02),
        "norm1_g": 1.0 + nrm(ks[4], (DEPTH, D), 0.02),
        "norm2_g": 1.0 + nrm(ks[5], (DEPTH, D), 0.02),
        "w_in": nrm(ks[6], (DEPTH, D, IN_WIDTH), D ** -0.5),
        "attn_sinks": nrm(ks[7], (DEPTH, ATTN_HEADS), 0.5),
        "rel_bias": nrm(ks[8], (N_BUCKETS, ATTN_HEADS), 0.5),
        "gla_w_gk2": nrm(ks[9], (DEPTH, GLA_RANK, GLA_K), GLA_RANK ** -0.5),
        "gla_b_gk2": nrm(ks[10], (DEPTH, GLA_K), 0.02),
        "gla_norm_g": 1.0 + nrm(ks[11], (DEPTH, GLA_DV), 0.02),
        "w_out": nrm(ks[12], (DEPTH, MIX_WIDTH, D), MIX_WIDTH ** -0.5),
        "peer_w_q": nrm(ks[13], (DEPTH, D, PEER_HEADS * PEER_DKEY), D ** -0.5),
        "peer_keys": nrm(ks[14], (DEPTH, PEER_HEADS, 2, N_KEYS, PEER_DKEY // 2), (PEER_DKEY // 2) ** -0.5),
        "peer_u": nrm(ks[15], (DEPTH, N_EXPERTS, D), D ** -0.5),
        "peer_v": nrm(ks[16], (DEPTH, N_EXPERTS, D), PEER_HEADS ** -0.5),
        "final_g": 1.0 + nrm(ks[17], (D,), 0.02),
    }


def reference(x, c, w_ada, b_ada, norm1_g, norm2_g, w_in, attn_sinks, rel_bias,
              gla_w_gk2, gla_b_gk2, gla_norm_g, w_out, peer_w_q, peer_keys,
              peer_u, peer_v, final_g):
    c_act = jax.nn.silu(c)
    for l in range(DEPTH):
        mod = c_act @ w_ada[l] + b_ada[l]
        x = hybrid_layer(x, mod.astype(x.dtype), norm1_g[l], norm2_g[l], w_in[l],
                         attn_sinks[l], rel_bias, gla_w_gk2[l], gla_b_gk2[l],
                         gla_norm_g[l], w_out[l], peer_w_q[l], peer_keys[l],
                         peer_u[l], peer_v[l])
    return rmsnorm(x, final_g)
```

```python
import functools
import math

import jax
import jax.numpy as jnp
import numpy as np
from jax import lax
from jax.experimental import pallas as pl
from jax.experimental.pallas import tpu as pltpu

BF = jnp.bfloat16
F32 = jnp.float32

D_MODEL = 4096
BATCH = 4
SEQ = 2048
NTOK = BATCH * SEQ
HEAD_DIM = 128
ATTN_HEADS = 16
ATTN_KV_HEADS = 2
ATTN_GROUP = 8
WINDOW = 128
BLOCK = 128
N_BUCKETS = 32
MAX_DISTANCE = 128
GLA_HEADS = 4
GLA_DV = 512
GLA_DK = 256
GLA_RANK = 16
GLA_TAU = 16.0
GLA_CHUNK = 64
ATTN_Q = 2048
ATTN_KV = 256
GLA_K = 1024
GLA_V = 2048
PEER_HEADS = 8
N_KEYS = 128
N_EXPERTS = N_KEYS * N_KEYS
PEER_TOPK = 16
PEER_DKEY = 256
N_MOD = 6
EPS = 1e-6
NEG_INF = -1e30

OFF_AQ, OFF_GQ, OFF_GK, OFF_GV, OFF_GOUT, OFF_AK, OFF_AV = 0, 2048, 3072, 4096, 6144, 8192, 8448
MAIN_WIDTH = 8704

VMEM_LIMIT = 56 * 1024 * 1024

NT_DIMS = (((1,), (1,)), ((), ()))
TN_DIMS = (((0,), (0,)), ((), ()))


def _params(*sem):
    return pltpu.CompilerParams(dimension_semantics=sem, vmem_limit_bytes=VMEM_LIMIT)


def _ada_kernel(c_ref, w_ref, b_ref, o_ref):
    c = c_ref[...]
    ca = c / (1.0 + jnp.exp(-c))
    o_ref[...] = jnp.dot(ca.astype(BF), w_ref[...].astype(BF), preferred_element_type=F32) + b_ref[...]


def _ada(c8, w_ada, b_ada):
    tn = 512
    n = N_MOD * D_MODEL
    return pl.pallas_call(
        _ada_kernel,
        grid=(n // tn,),
        in_specs=[pl.BlockSpec((8, D_MODEL), lambda j: (0, 0)),
                  pl.BlockSpec((D_MODEL, tn), lambda j: (0, j)),
                  pl.BlockSpec((1, tn), lambda j: (0, j))],
        out_specs=pl.BlockSpec((8, tn), lambda j: (0, j)),
        out_shape=jax.ShapeDtypeStruct((8, n), F32),
        compiler_params=_params("arbitrary"),
        name="ada",
    )(c8, w_ada, b_ada)


def _norm_mod_kernel(x_ref, g_ref, sc_ref, sh_ref, o_ref):
    x = x_ref[...]
    ms = jnp.mean(x * x, axis=-1, keepdims=True)
    y = x * lax.rsqrt(ms + EPS) * g_ref[...]
    o_ref[...] = (y * (1.0 + sc_ref[...]) + sh_ref[...]).astype(o_ref.dtype)


def _norm_mod(x2, g, modr, scale_idx, shift_idx):
    tr = 256
    return pl.pallas_call(
        _norm_mod_kernel,
        grid=(NTOK // tr,),
        in_specs=[pl.BlockSpec((tr, D_MODEL), lambda i: (i, 0)),
                  pl.BlockSpec((1, D_MODEL), lambda i: (0, 0)),
                  pl.BlockSpec((None, 1, D_MODEL), lambda i: ((i * tr) // SEQ * N_MOD + scale_idx, 0, 0)),
                  pl.BlockSpec((None, 1, D_MODEL), lambda i: ((i * tr) // SEQ * N_MOD + shift_idx, 0, 0))],
        out_specs=pl.BlockSpec((tr, D_MODEL), lambda i: (i, 0)),
        out_shape=jax.ShapeDtypeStruct((NTOK, D_MODEL), BF),
        compiler_params=_params("arbitrary"),
        name="norm_mod",
    )(x2, g, modr, modr)


def _mm_kernel(a_ref, w_ref, o_ref):
    o_ref[...] = jnp.dot(a_ref[...], w_ref[...], preferred_element_type=F32).astype(o_ref.dtype)


def _matmul(a, w, out_dtype, tm, tn, name):
    m, k = a.shape
    n = w.shape[1]
    return pl.pallas_call(
        _mm_kernel,
        grid=(m // tm, n // tn),
        in_specs=[pl.BlockSpec((tm, k), lambda i, j: (i, 0)),
                  pl.BlockSpec((k, tn), lambda i, j: (0, j))],
        out_specs=pl.BlockSpec((tm, tn), lambda i, j: (i, j)),
        out_shape=jax.ShapeDtypeStruct((m, n), out_dtype),
        compiler_params=_params("arbitrary", "arbitrary"),
        name=name,
    )(a, w)


def _t5_bucket(dist):
    max_exact = N_BUCKETS // 2
    d = jnp.maximum(dist, 0)
    log_ratio = jnp.log(jnp.maximum(d, 1).astype(F32) / max_exact) / math.log(MAX_DISTANCE / max_exact)
    large = max_exact + (log_ratio * (N_BUCKETS - max_exact)).astype(jnp.int32)
    large = jnp.minimum(large, N_BUCKETS - 1)
    return jnp.where(d < max_exact, d, large)


def _bias_kernel(rel_ref, sink_ref, bucket_ref, o_ref):
    h = pl.program_id(0)
    bucket = bucket_ref[...]
    acc = jnp.full((BLOCK, 2 * BLOCK), NEG_INF, F32)
    for b in range(N_BUCKETS):
        acc = jnp.where(bucket == b, rel_ref[b, h], acc)
    col = lax.broadcasted_iota(jnp.int32, (BLOCK, 2 * BLOCK), 1)
    acc = jnp.where(col == 0, sink_ref[h], acc)
    o_ref[0, 0] = acc
    o_ref[1, 0] = jnp.where(jnp.logical_and(col < BLOCK, col > 0), NEG_INF, acc)


def _bias_table(rel_bias, sinks):
    qi = jnp.arange(BLOCK)[:, None]
    kj = jnp.arange(2 * BLOCK)[None, :]
    dist = qi + BLOCK - kj
    in_window = (dist >= 0) & (dist < WINDOW)
    bucket = jnp.where(in_window, _t5_bucket(dist), -1).astype(jnp.int32)
    tab = pl.pallas_call(
        _bias_kernel,
        grid=(ATTN_HEADS,),
        in_specs=[pl.BlockSpec(memory_space=pltpu.SMEM),
                  pl.BlockSpec(memory_space=pltpu.SMEM),
                  pl.BlockSpec((BLOCK, 2 * BLOCK), lambda h: (0, 0))],
        out_specs=pl.BlockSpec((2, 1, BLOCK, 2 * BLOCK), lambda h: (0, h, 0, 0)),
        out_shape=jax.ShapeDtypeStruct((2, ATTN_HEADS, BLOCK, 2 * BLOCK), F32),
        compiler_params=_params("arbitrary"),
        name="bias_table",
    )(rel_bias, sinks, bucket)
    return tab.reshape(2, ATTN_KV_HEADS, ATTN_GROUP * BLOCK, 2 * BLOCK)


def _attn_kernel(q_ref, kp_ref, kc_ref, vp_ref, vc_ref, bias_ref, o_ref):
    q = q_ref[...]
    qs = jnp.concatenate([q[:, g * HEAD_DIM:(g + 1) * HEAD_DIM] for g in range(ATTN_GROUP)], axis=0)
    row = lax.broadcasted_iota(jnp.int32, (BLOCK, HEAD_DIM), 0)
    kp = jnp.where(row == 0, 0.0, kp_ref[...].astype(F32)).astype(BF)
    vp = jnp.where(row == 0, 0.0, vp_ref[...].astype(F32)).astype(BF)
    kcat = jnp.concatenate([kp, kc_ref[...]], axis=0)
    vcat = jnp.concatenate([vp, vc_ref[...]], axis=0)
    s = lax.dot_general(qs, kcat, NT_DIMS, preferred_element_type=F32) * (HEAD_DIM ** -0.5) + bias_ref[...]
    m = jnp.max(s, axis=-1, keepdims=True)
    p = jnp.exp(s - m).astype(BF)
    vaug = jnp.concatenate([vcat, jnp.ones((2 * BLOCK, HEAD_DIM), BF)], axis=1)
    oa = jnp.dot(p, vaug, preferred_element_type=F32)
    o = oa[:, :HEAD_DIM] / oa[:, HEAD_DIM:]
    o_ref[...] = jnp.concatenate([o[g * BLOCK:(g + 1) * BLOCK, :] for g in range(ATTN_GROUP)], axis=1).astype(o_ref.dtype)


def _attention(proj, bias_tab):
    nb = SEQ // BLOCK
    gw = ATTN_GROUP * HEAD_DIM
    kblk = OFF_AK // HEAD_DIM
    vblk = OFF_AV // HEAD_DIM
    return pl.pallas_call(
        _attn_kernel,
        grid=(ATTN_KV_HEADS, BATCH, nb),
        in_specs=[pl.BlockSpec((BLOCK, gw), lambda h, b, n: (b * nb + n, h)),
                  pl.BlockSpec((BLOCK, HEAD_DIM), lambda h, b, n: (b * nb + jnp.maximum(n - 1, 0), kblk + h)),
                  pl.BlockSpec((BLOCK, HEAD_DIM), lambda h, b, n: (b * nb + n, kblk + h)),
                  pl.BlockSpec((BLOCK, HEAD_DIM), lambda h, b, n: (b * nb + jnp.maximum(n - 1, 0), vblk + h)),
                  pl.BlockSpec((BLOCK, HEAD_DIM), lambda h, b, n: (b * nb + n, vblk + h)),
                  pl.BlockSpec((None, None, gw, 2 * BLOCK), lambda h, b, n: (jnp.where(n == 0, 1, 0), h, 0, 0))],
        out_specs=pl.BlockSpec((BLOCK, gw), lambda h, b, n: (b * nb + n, h)),
        out_shape=jax.ShapeDtypeStruct((NTOK, ATTN_Q), BF),
        compiler_params=_params("arbitrary", "arbitrary", "arbitrary"),
        name="swa_attn",
    )(proj, proj, proj, proj, proj, bias_tab)


GLA_TB = 128


def _gla_kernel(q_ref, k_ref, v_ref, go_ref, glow_ref, w2_ref, b2_ref, g_ref, o_ref, st_ref):
    @pl.when(pl.program_id(1) == 0)
    def _():
        st_ref[...] = jnp.zeros_like(st_ref)

    C = GLA_CHUNK
    ri = lax.broadcasted_iota(jnp.int32, (C, C), 0)
    ci = lax.broadcasted_iota(jnp.int32, (C, C), 1)
    tril = ri >= ci
    trilf = tril.astype(F32)
    for c in range(GLA_TB // C):
        rs = slice(c * C, (c + 1) * C)
        z = jnp.dot(glow_ref[rs, :], w2_ref[...], precision=lax.Precision.HIGHEST,
                    preferred_element_type=F32) + b2_ref[...]
        la = (jnp.minimum(z, 0.0) - jnp.log1p(jnp.exp(-jnp.abs(z)))) * (1.0 / GLA_TAU)
        bcum = jnp.dot(trilf, la, precision=lax.Precision.HIGHEST, preferred_element_type=F32)
        blast = bcum[C - 1:C, :]
        eb = jnp.exp(bcum)
        enb = jnp.exp(-bcum)
        edl = jnp.exp(blast - bcum)
        el = jnp.exp(blast)
        for h in range(GLA_HEADS):
            ks = slice(h * GLA_DK, (h + 1) * GLA_DK)
            vs = slice(h * GLA_DV, (h + 1) * GLA_DV)
            q = q_ref[rs, ks].astype(F32) * (GLA_DK ** -0.5)
            k = k_ref[rs, ks].astype(F32)
            v = v_ref[rs, vs]
            qd = (q * eb[:, ks]).astype(BF)
            ki = (k * enb[:, ks]).astype(BF)
            kd = (k * edl[:, ks]).astype(BF)
            a = lax.dot_general(qd, ki, NT_DIMS, preferred_element_type=F32)
            a = jnp.where(tril, a, 0.0).astype(BF)
            st = st_ref[h]
            o = (jnp.dot(a, v, preferred_element_type=F32)
                 + lax.dot_general(qd, st.astype(BF), NT_DIMS, preferred_element_type=F32))
            st_ref[h] = st * el[:, ks] + lax.dot_general(v, kd, TN_DIMS, preferred_element_type=F32)
            ms = jnp.mean(o * o, axis=-1, keepdims=True)
            y = o * lax.rsqrt(ms + EPS) * g_ref[...]
            gate = go_ref[rs, vs].astype(F32)
            y = y * (gate / (1.0 + jnp.exp(-gate)))
            o_ref[rs, vs] = y.astype(o_ref.dtype)


def _gla(proj, glow, w2p, b2, gnorm):
    nt = SEQ // GLA_TB
    return pl.pallas_call(
        _gla_kernel,
        grid=(BATCH, nt),
        in_specs=[pl.BlockSpec((GLA_TB, GLA_K), lambda b, t: (b * nt + t, OFF_GQ // GLA_K)),
                  pl.BlockSpec((GLA_TB, GLA_K), lambda b, t: (b * nt + t, OFF_GK // GLA_K)),
                  pl.BlockSpec((GLA_TB, GLA_V), lambda b, t: (b * nt + t, OFF_GV // GLA_V)),
                  pl.BlockSpec((GLA_TB, GLA_V), lambda b, t: (b * nt + t, OFF_GOUT // GLA_V)),
                  pl.BlockSpec((GLA_TB, 128), lambda b, t: (b * nt + t, 0)),
                  pl.BlockSpec((128, GLA_K), lambda b, t: (0, 0)),
                  pl.BlockSpec((1, GLA_K), lambda b, t: (0, 0)),
                  pl.BlockSpec((1, GLA_DV), lambda b, t: (0, 0))],
        out_specs=pl.BlockSpec((GLA_TB, GLA_V), lambda b, t: (b * nt + t, 0)),
        out_shape=jax.ShapeDtypeStruct((NTOK, GLA_V), BF),
        scratch_shapes=[pltpu.VMEM((GLA_HEADS, GLA_DV, GLA_DK), F32)],
        compiler_params=_params("arbitrary", "arbitrary"),
        name="gla",
    )(proj, proj, proj, proj, glow, w2p, b2, gnorm)


def _outproj_kernel(a_ref, g_ref, w_ref, x_ref, gate_ref, o_ref):
    acc = (jnp.dot(a_ref[...], w_ref[0:ATTN_Q, :], preferred_element_type=F32)
           + jnp.dot(g_ref[...], w_ref[ATTN_Q:ATTN_Q + GLA_V, :], preferred_element_type=F32))
    o_ref[...] = x_ref[...] + gate_ref[...] * acc


def _outproj(attn, gla, w_out, x2, modr):
    tm, tn = 1024, 512
    return pl.pallas_call(
        _outproj_kernel,
        grid=(NTOK // tm, D_MODEL // tn),
        in_specs=[pl.BlockSpec((tm, ATTN_Q), lambda i, j: (i, 0)),
                  pl.BlockSpec((tm, GLA_V), lambda i, j: (i, 0)),
                  pl.BlockSpec((ATTN_Q + GLA_V, tn), lambda i, j: (0, j)),
                  pl.BlockSpec((tm, tn), lambda i, j: (i, j)),
                  pl.BlockSpec((None, 1, tn), lambda i, j: ((i * tm) // SEQ * N_MOD + 2, 0, j))],
        out_specs=pl.BlockSpec((tm, tn), lambda i, j: (i, j)),
        out_shape=jax.ShapeDtypeStruct((NTOK, D_MODEL), F32),
        compiler_params=_params("arbitrary", "arbitrary"),
        name="outproj",
    )(attn, gla, w_out, x2, modr)


ROUTE_TR = 256
NOT_TOP = 99.0
CAND_ROWS = 8 * (PEER_TOPK + 1)


def _topk_ranked(s):
    t = s.shape[1]
    rows = lax.broadcasted_iota(jnp.int32, s.shape, 0).astype(F32)
    work = s
    rank = jnp.full(s.shape, NOT_TOP, F32)
    vals = []
    for r in range(PEER_TOPK):
        m = jnp.max(work, axis=0, keepdims=True)
        idx = jnp.min(jnp.where(work == m, rows, float(N_KEYS)), axis=0, keepdims=True)
        sel = rows == idx
        rank = jnp.where(sel, float(r), rank)
        work = jnp.where(sel, -jnp.inf, work)
        vals.append(m)
    return vals, rank


def _route_kernel(q_ref, keys_ref, k2_ref, e2_ref, n1_ref, e1_ref):
    q = q_ref[...].astype(BF)
    s1 = lax.dot_general(keys_ref[0], q[:, :N_KEYS], NT_DIMS, preferred_element_type=F32)
    s2 = lax.dot_general(keys_ref[1], q[:, N_KEYS:], NT_DIMS, preferred_element_type=F32)
    v1, rank1 = _topk_ranked(s1)
    v2, rank2 = _topk_ranked(s2)
    t = s1.shape[1]
    t2a = jnp.concatenate(v2[:8], axis=0)
    t2b = jnp.concatenate(v2[8:], axis=0)
    cand = jnp.concatenate([v1[0] + t2a, v1[0] + t2b] + [v1[r] + t2a for r in range(1, PEER_TOPK)], axis=0)
    i = lax.broadcasted_iota(jnp.int32, (CAND_ROWS, t), 0)
    flat = jnp.where(i < 16, i, 2 * i - (i & 7) - 16).astype(F32)
    work = cand
    for _ in range(PEER_TOPK):
        m = jnp.max(work, axis=0, keepdims=True)
        idx = jnp.min(jnp.where(work == m, flat, 1e9), axis=0, keepdims=True)
        work = jnp.where(flat == idx, -jnp.inf, work)
    chosen = work == -jnp.inf
    top = v1[0] + v2[0]
    z = jnp.sum(jnp.where(chosen, jnp.exp(cand - top), 0.0), axis=0, keepdims=True)
    cnt = jnp.where(chosen, 1.0, 0.0)
    n = [jnp.sum(cnt[0:16], axis=0, keepdims=True)]
    n += [jnp.sum(cnt[8 * (r + 1):8 * (r + 2)], axis=0, keepdims=True) for r in range(1, PEER_TOPK)]
    n1 = jnp.zeros_like(s1)
    for r in range(PEER_TOPK):
        n1 = jnp.where(rank1 == float(r), n[r], n1)
    k2_ref[...] = rank2
    e2_ref[...] = jnp.exp(s2 - v2[0])
    n1_ref[...] = n1
    e1_ref[...] = jnp.exp(s1 - v1[0]) / z


def _route(q, keys):
    tr = ROUTE_TR
    shp = jax.ShapeDtypeStruct((PEER_HEADS, N_KEYS, NTOK), F32)
    ospec = pl.BlockSpec((None, N_KEYS, tr), lambda i, h: (h, 0, i))
    return pl.pallas_call(
        _route_kernel,
        grid=(NTOK // tr, PEER_HEADS),
        in_specs=[pl.BlockSpec((tr, PEER_DKEY), lambda i, h: (i, h)),
                  pl.BlockSpec((None, 2, N_KEYS, PEER_DKEY // 2), lambda i, h: (h, 0, 0, 0))],
        out_specs=[ospec, ospec, ospec, ospec],
        out_shape=[shp, shp, shp, shp],
        compiler_params=_params("arbitrary", "arbitrary"),
        name="peer_route",
    )(q, keys)


PEER_TM = 512
PEER_TE = 512


def _peer_kernel(h_ref, u_ref, v_ref, k2_ref, e2_ref, n1_ref, e1_ref, o_ref):
    j = pl.program_id(1)
    act = lax.dot_general(u_ref[...], h_ref[...], NT_DIMS, preferred_element_type=F32)
    parts = []
    for aa in range(PEER_TE // N_KEYS):
        a = j * (PEER_TE // N_KEYS) + aa
        g = jnp.zeros((N_KEYS, PEER_TM), F32)
        for h in range(PEER_HEADS):
            n1 = n1_ref[h, pl.ds(a, 1), :]
            e1 = e1_ref[h, pl.ds(a, 1), :]
            g = g + jnp.where(k2_ref[h] < n1, e2_ref[h], 0.0) * e1
        x = act[aa * N_KEYS:(aa + 1) * N_KEYS, :]
        ge = 0.5 * x * (1.0 + lax.erf(x * (2.0 ** -0.5)))
        parts.append((g * ge).astype(BF))
    wt = jnp.concatenate(parts, axis=0)
    contrib = lax.dot_general(wt, v_ref[...], TN_DIMS, preferred_element_type=F32)

    @pl.when(j == 0)
    def _():
        o_ref[...] = contrib

    @pl.when(j > 0)
    def _():
        o_ref[...] += contrib


def _peer(h2, u, v, k2, e2, n1, e1):
    tm, te = PEER_TM, PEER_TE
    rspec = pl.BlockSpec((PEER_HEADS, N_KEYS, tm), lambda i, j: (0, 0, i), pipeline_mode=pl.Buffered(1))
    return pl.pallas_call(
        _peer_kernel,
        grid=(NTOK // tm, N_EXPERTS // te),
        in_specs=[pl.BlockSpec((tm, D_MODEL), lambda i, j: (i, 0), pipeline_mode=pl.Buffered(1)),
                  pl.BlockSpec((te, D_MODEL), lambda i, j: (j, 0)),
                  pl.BlockSpec((te, D_MODEL), lambda i, j: (j, 0)),
                  rspec, rspec, rspec, rspec],
        out_specs=pl.BlockSpec((tm, D_MODEL), lambda i, j: (i, 0)),
        out_shape=jax.ShapeDtypeStruct((NTOK, D_MODEL), F32),
        compiler_params=_params("arbitrary", "arbitrary"),
        name="peer_dense",
    )(h2, u, v, k2, e2, n1, e1)


def _final_kernel(x_ref, p_ref, gate_ref, g_ref, o_ref):
    y = x_ref[...] + gate_ref[...] * p_ref[...]
    ms = jnp.mean(y * y, axis=-1, keepdims=True)
    o_ref[...] = y * lax.rsqrt(ms + EPS) * g_ref[...]


def _final(x1, peer_out, modr, final_g):
    tr = 256
    return pl.pallas_call(
        _final_kernel,
        grid=(NTOK // tr,),
        in_specs=[pl.BlockSpec((tr, D_MODEL), lambda i: (i, 0)),
                  pl.BlockSpec((tr, D_MODEL), lambda i: (i, 0)),
                  pl.BlockSpec((None, 1, D_MODEL), lambda i: ((i * tr) // SEQ * N_MOD + 5, 0, 0)),
                  pl.BlockSpec((1, D_MODEL), lambda i: (0, 0))],
        out_specs=pl.BlockSpec((tr, D_MODEL), lambda i: (i, 0)),
        out_shape=jax.ShapeDtypeStruct((NTOK, D_MODEL), F32),
        compiler_params=_params("arbitrary"),
        name="final_norm",
    )(x1, peer_out, modr, final_g)


def kernel(x, c, w_ada, b_ada, norm1_g, norm2_g, w_in, attn_sinks, rel_bias, gla_w_gk2, gla_b_gk2,
           gla_norm_g, w_out, peer_w_q, peer_keys, peer_u, peer_v, final_g):
    assert x.shape == (BATCH, SEQ, D_MODEL) and w_ada.shape[0] == 1
    x2 = x.reshape(NTOK, D_MODEL)
    c8 = jnp.pad(c, ((0, 8 - BATCH), (0, 0)))
    mod = _ada(c8, w_ada[0], b_ada[0][None, :])
    modr = mod[:BATCH].reshape(BATCH * N_MOD, 1, D_MODEL)

    wi = w_in[0]
    o = np.cumsum([0, ATTN_Q, ATTN_KV, ATTN_KV, GLA_K, GLA_K, GLA_V, GLA_RANK, GLA_V])
    aq, ak, av, gq, gk, gv, glo, gout = [wi[:, o[i]:o[i + 1]] for i in range(8)]
    w_main = jnp.concatenate([aq, gq, gk, gv, gout, ak, av], axis=1).astype(BF)
    w_glow = jnp.pad(glo, ((0, 0), (0, 128 - GLA_RANK))).astype(BF)
    w2p = jnp.pad(gla_w_gk2[0], ((0, 128 - GLA_RANK), (0, 0)))

    h1 = _norm_mod(x2, norm1_g[0][None, :], modr, 1, 0)
    proj = _matmul(h1, w_main, BF, 1024, 512, "in_proj")
    glow = _matmul(h1, w_glow, F32, 1024, 128, "glow_proj")

    bias_tab = _bias_table(rel_bias, attn_sinks[0])
    attn = _attention(proj, bias_tab)
    gla = _gla(proj, glow, w2p, gla_b_gk2[0][None, :], gla_norm_g[0][None, :])
    x1 = _outproj(attn, gla, w_out[0].astype(BF), x2, modr)

    h2 = _norm_mod(x1, norm2_g[0][None, :], modr, 4, 3)
    q = _matmul(h2, peer_w_q[0].astype(BF), F32, 1024, 512, "peer_q")
    k2, e2, n1, e1 = _route(q, peer_keys[0].astype(BF))
    peer_out = _peer(h2, peer_u[0].astype(BF), peer_v[0].astype(BF), k2, e2, n1, e1)
    out = _final(x1, peer_out, modr, final_g[None, :])
    return out.reshape(BATCH, SEQ, D_MODEL)
```

```python
import functools
import math

import jax
import jax.numpy as jnp
import numpy as np
from jax import lax
from jax.experimental import pallas as pl
from jax.experimental.pallas import tpu as pltpu

BF = jnp.bfloat16
F32 = jnp.float32

D_MODEL = 4096
BATCH = 4
SEQ = 2048
NTOK = BATCH * SEQ
HEAD_DIM = 128
ATTN_HEADS = 16
ATTN_KV_HEADS = 2
ATTN_GROUP = 8
WINDOW = 128
BLOCK = 128
N_BUCKETS = 32
MAX_DISTANCE = 128
GLA_HEADS = 4
GLA_DV = 512
GLA_DK = 256
GLA_RANK = 16
GLA_TAU = 16.0
GLA_CHUNK = 64
ATTN_Q = 2048
ATTN_KV = 256
GLA_K = 1024
GLA_V = 2048
PEER_HEADS = 8
N_KEYS = 128
N_EXPERTS = N_KEYS * N_KEYS
PEER_TOPK = 16
PEER_DKEY = 256
N_MOD = 6
EPS = 1e-6
NEG_INF = -1e30

OFF_AQ, OFF_GQ, OFF_GK, OFF_GV, OFF_GOUT, OFF_AK, OFF_AV = 0, 2048, 3072, 4096, 6144, 8192, 8448
MAIN_WIDTH = 8704

VMEM_LIMIT = 56 * 1024 * 1024

NT_DIMS = (((1,), (1,)), ((), ()))
TN_DIMS = (((0,), (0,)), ((), ()))


def _params(*sem, flags=None):
    return pltpu.CompilerParams(dimension_semantics=sem, vmem_limit_bytes=VMEM_LIMIT, flags=flags)


def _ada_kernel(c_ref, w_ref, b_ref, o_ref):
    c = c_ref[...]
    ca = c / (1.0 + jnp.exp(-c))
    o_ref[...] = jnp.dot(ca.astype(BF), w_ref[...].astype(BF), preferred_element_type=F32) + b_ref[...]


def _ada(c8, w_ada, b_ada):
    tn = 512
    n = N_MOD * D_MODEL
    return pl.pallas_call(
        _ada_kernel,
        grid=(n // tn,),
        in_specs=[pl.BlockSpec((8, D_MODEL), lambda j: (0, 0)),
                  pl.BlockSpec((D_MODEL, tn), lambda j: (0, j)),
                  pl.BlockSpec((1, tn), lambda j: (0, j))],
        out_specs=pl.BlockSpec((8, tn), lambda j: (0, j)),
        out_shape=jax.ShapeDtypeStruct((8, n), F32),
        compiler_params=_params("arbitrary"),
        name="ada",
    )(c8, w_ada, b_ada)


def _norm_mod_kernel(x_ref, g_ref, sc_ref, sh_ref, o_ref):
    x = x_ref[...]
    ms = jnp.mean(x * x, axis=-1, keepdims=True)
    y = x * lax.rsqrt(ms + EPS) * g_ref[...]
    o_ref[...] = (y * (1.0 + sc_ref[...]) + sh_ref[...]).astype(o_ref.dtype)


def _norm_mod(x2, g, modr, scale_idx, shift_idx):
    tr = 256
    return pl.pallas_call(
        _norm_mod_kernel,
        grid=(NTOK // tr,),
        in_specs=[pl.BlockSpec((tr, D_MODEL), lambda i: (i, 0)),
                  pl.BlockSpec((1, D_MODEL), lambda i: (0, 0)),
                  pl.BlockSpec((None, 1, D_MODEL), lambda i: ((i * tr) // SEQ * N_MOD + scale_idx, 0, 0)),
                  pl.BlockSpec((None, 1, D_MODEL), lambda i: ((i * tr) // SEQ * N_MOD + shift_idx, 0, 0))],
        out_specs=pl.BlockSpec((tr, D_MODEL), lambda i: (i, 0)),
        out_shape=jax.ShapeDtypeStruct((NTOK, D_MODEL), BF),
        compiler_params=_params("arbitrary"),
        name="norm_mod",
    )(x2, g, modr, modr)


def _mm_kernel(a_ref, w_ref, o_ref):
    o_ref[...] = jnp.dot(a_ref[...], w_ref[...], preferred_element_type=F32).astype(o_ref.dtype)


def _matmul(a, w, out_dtype, tm, tn, name):
    m, k = a.shape
    n = w.shape[1]
    return pl.pallas_call(
        _mm_kernel,
        grid=(m // tm, n // tn),
        in_specs=[pl.BlockSpec((tm, k), lambda i, j: (i, 0)),
                  pl.BlockSpec((k, tn), lambda i, j: (0, j))],
        out_specs=pl.BlockSpec((tm, tn), lambda i, j: (i, j)),
        out_shape=jax.ShapeDtypeStruct((m, n), out_dtype),
        compiler_params=_params("arbitrary", "arbitrary"),
        name=name,
    )(a, w)


def _t5_bucket(dist):
    max_exact = N_BUCKETS // 2
    d = jnp.maximum(dist, 0)
    log_ratio = jnp.log(jnp.maximum(d, 1).astype(F32) / max_exact) / math.log(MAX_DISTANCE / max_exact)
    large = max_exact + (log_ratio * (N_BUCKETS - max_exact)).astype(jnp.int32)
    large = jnp.minimum(large, N_BUCKETS - 1)
    return jnp.where(d < max_exact, d, large)


def _bias_kernel(rel_ref, sink_ref, bucket_ref, o_ref):
    h = pl.program_id(0)
    bucket = bucket_ref[...]
    acc = jnp.full((BLOCK, 2 * BLOCK), NEG_INF, F32)
    for b in range(N_BUCKETS):
        acc = jnp.where(bucket == b, rel_ref[b, h], acc)
    col = lax.broadcasted_iota(jnp.int32, (BLOCK, 2 * BLOCK), 1)
    acc = jnp.where(col == 0, sink_ref[h], acc)
    o_ref[0, 0] = acc
    o_ref[1, 0] = jnp.where(jnp.logical_and(col < BLOCK, col > 0), NEG_INF, acc)


def _bias_table(rel_bias, sinks):
    qi = jnp.arange(BLOCK)[:, None]
    kj = jnp.arange(2 * BLOCK)[None, :]
    dist = qi + BLOCK - kj
    in_window = (dist >= 0) & (dist < WINDOW)
    bucket = jnp.where(in_window, _t5_bucket(dist), -1).astype(jnp.int32)
    tab = pl.pallas_call(
        _bias_kernel,
        grid=(ATTN_HEADS,),
        in_specs=[pl.BlockSpec(memory_space=pltpu.SMEM),
                  pl.BlockSpec(memory_space=pltpu.SMEM),
                  pl.BlockSpec((BLOCK, 2 * BLOCK), lambda h: (0, 0))],
        out_specs=pl.BlockSpec((2, 1, BLOCK, 2 * BLOCK), lambda h: (0, h, 0, 0)),
        out_shape=jax.ShapeDtypeStruct((2, ATTN_HEADS, BLOCK, 2 * BLOCK), F32),
        compiler_params=_params("arbitrary"),
        name="bias_table",
    )(rel_bias, sinks, bucket)
    return tab.reshape(2, ATTN_KV_HEADS, ATTN_GROUP * BLOCK, 2 * BLOCK)


def _attn_kernel(q_ref, kp_ref, kc_ref, vp_ref, vc_ref, bias_ref, o_ref):
    q = q_ref[...]
    qs = jnp.concatenate([q[:, g * HEAD_DIM:(g + 1) * HEAD_DIM] for g in range(ATTN_GROUP)], axis=0)
    row = lax.broadcasted_iota(jnp.int32, (BLOCK, HEAD_DIM), 0)
    kp = jnp.where(row == 0, 0.0, kp_ref[...].astype(F32)).astype(BF)
    vp = jnp.where(row == 0, 0.0, vp_ref[...].astype(F32)).astype(BF)
    kcat = jnp.concatenate([kp, kc_ref[...]], axis=0)
    vcat = jnp.concatenate([vp, vc_ref[...]], axis=0)
    s = lax.dot_general(qs, kcat, NT_DIMS, preferred_element_type=F32) * (HEAD_DIM ** -0.5) + bias_ref[...]
    m = jnp.max(s, axis=-1, keepdims=True)
    p = jnp.exp(s - m).astype(BF)
    vaug = jnp.concatenate([vcat, jnp.ones((2 * BLOCK, HEAD_DIM), BF)], axis=1)
    oa = jnp.dot(p, vaug, preferred_element_type=F32)
    o = oa[:, :HEAD_DIM] / oa[:, HEAD_DIM:]
    o_ref[...] = jnp.concatenate([o[g * BLOCK:(g + 1) * BLOCK, :] for g in range(ATTN_GROUP)], axis=1).astype(o_ref.dtype)


def _attention(proj, bias_tab):
    nb = SEQ // BLOCK
    gw = ATTN_GROUP * HEAD_DIM
    kblk = OFF_AK // HEAD_DIM
    vblk = OFF_AV // HEAD_DIM
    return pl.pallas_call(
        _attn_kernel,
        grid=(ATTN_KV_HEADS, BATCH, nb),
        in_specs=[pl.BlockSpec((BLOCK, gw), lambda h, b, n: (b * nb + n, h)),
                  pl.BlockSpec((BLOCK, HEAD_DIM), lambda h, b, n: (b * nb + jnp.maximum(n - 1, 0), kblk + h)),
                  pl.BlockSpec((BLOCK, HEAD_DIM), lambda h, b, n: (b * nb + n, kblk + h)),
                  pl.BlockSpec((BLOCK, HEAD_DIM), lambda h, b, n: (b * nb + jnp.maximum(n - 1, 0), vblk + h)),
                  pl.BlockSpec((BLOCK, HEAD_DIM), lambda h, b, n: (b * nb + n, vblk + h)),
                  pl.BlockSpec((None, None, gw, 2 * BLOCK), lambda h, b, n: (jnp.where(n == 0, 1, 0), h, 0, 0))],
        out_specs=pl.BlockSpec((BLOCK, gw), lambda h, b, n: (b * nb + n, h)),
        out_shape=jax.ShapeDtypeStruct((NTOK, ATTN_Q), BF),
        compiler_params=_params("arbitrary", "arbitrary", "arbitrary"),
        name="swa_attn",
    )(proj, proj, proj, proj, proj, bias_tab)


GLA_TB = 128


def _gla_kernel(q_ref, k_ref, v_ref, go_ref, glow_ref, w2_ref, b2_ref, g_ref, o_ref, st_ref):
    @pl.when(pl.program_id(1) == 0)
    def _():
        st_ref[...] = jnp.zeros_like(st_ref)

    C = GLA_CHUNK
    ri = lax.broadcasted_iota(jnp.int32, (C, C), 0)
    ci = lax.broadcasted_iota(jnp.int32, (C, C), 1)
    tril = ri >= ci
    trilf = tril.astype(F32)
    for c in range(GLA_TB // C):
        rs = slice(c * C, (c + 1) * C)
        z = jnp.dot(glow_ref[rs, :], w2_ref[...], precision=lax.Precision.HIGHEST,
                    preferred_element_type=F32) + b2_ref[...]
        la = (jnp.minimum(z, 0.0) - jnp.log1p(jnp.exp(-jnp.abs(z)))) * (1.0 / GLA_TAU)
        bcum = jnp.dot(trilf, la, precision=lax.Precision.HIGHEST, preferred_element_type=F32)
        blast = bcum[C - 1:C, :]
        eb = jnp.exp(bcum)
        enb = jnp.exp(-bcum)
        edl = jnp.exp(blast - bcum)
        el = jnp.exp(blast)
        for h in range(GLA_HEADS):
            ks = slice(h * GLA_DK, (h + 1) * GLA_DK)
            vs = slice(h * GLA_DV, (h + 1) * GLA_DV)
            q = q_ref[rs, ks].astype(F32) * (GLA_DK ** -0.5)
            k = k_ref[rs, ks].astype(F32)
            v = v_ref[rs, vs]
            qd = (q * eb[:, ks]).astype(BF)
            ki = (k * enb[:, ks]).astype(BF)
            kd = (k * edl[:, ks]).astype(BF)
            a = lax.dot_general(qd, ki, NT_DIMS, preferred_element_type=F32)
            a = jnp.where(tril, a, 0.0).astype(BF)
            st = st_ref[h]
            o = (jnp.dot(a, v, preferred_element_type=F32)
                 + lax.dot_general(qd, st.astype(BF), NT_DIMS, preferred_element_type=F32))
            st_ref[h] = st * el[:, ks] + lax.dot_general(v, kd, TN_DIMS, preferred_element_type=F32)
            ms = jnp.mean(o * o, axis=-1, keepdims=True)
            y = o * lax.rsqrt(ms + EPS) * g_ref[...]
            gate = go_ref[rs, vs].astype(F32)
            y = y * (gate / (1.0 + jnp.exp(-gate)))
            o_ref[rs, vs] = y.astype(o_ref.dtype)


def _gla(proj, glow, w2p, b2, gnorm):
    nt = SEQ // GLA_TB
    return pl.pallas_call(
        _gla_kernel,
        grid=(BATCH, nt),
        in_specs=[pl.BlockSpec((GLA_TB, GLA_K), lambda b, t: (b * nt + t, OFF_GQ // GLA_K)),
                  pl.BlockSpec((GLA_TB, GLA_K), lambda b, t: (b * nt + t, OFF_GK // GLA_K)),
                  pl.BlockSpec((GLA_TB, GLA_V), lambda b, t: (b * nt + t, OFF_GV // GLA_V)),
                  pl.BlockSpec((GLA_TB, GLA_V), lambda b, t: (b * nt + t, OFF_GOUT // GLA_V)),
                  pl.BlockSpec((GLA_TB, 128), lambda b, t: (b * nt + t, 0)),
                  pl.BlockSpec((128, GLA_K), lambda b, t: (0, 0)),
                  pl.BlockSpec((1, GLA_K), lambda b, t: (0, 0)),
                  pl.BlockSpec((1, GLA_DV), lambda b, t: (0, 0))],
        out_specs=pl.BlockSpec((GLA_TB, GLA_V), lambda b, t: (b * nt + t, 0)),
        out_shape=jax.ShapeDtypeStruct((NTOK, GLA_V), BF),
        scratch_shapes=[pltpu.VMEM((GLA_HEADS, GLA_DV, GLA_DK), F32)],
        compiler_params=_params("arbitrary", "arbitrary"),
        name="gla",
    )(proj, proj, proj, proj, glow, w2p, b2, gnorm)


def _outproj_kernel(a_ref, g_ref, w_ref, x_ref, gate_ref, o_ref):
    acc = (jnp.dot(a_ref[...], w_ref[0:ATTN_Q, :], preferred_element_type=F32)
           + jnp.dot(g_ref[...], w_ref[ATTN_Q:ATTN_Q + GLA_V, :], preferred_element_type=F32))
    o_ref[...] = x_ref[...] + gate_ref[...] * acc


def _outproj(attn, gla, w_out, x2, modr):
    tm, tn = 1024, 512
    return pl.pallas_call(
        _outproj_kernel,
        grid=(NTOK // tm, D_MODEL // tn),
        in_specs=[pl.BlockSpec((tm, ATTN_Q), lambda i, j: (i, 0)),
                  pl.BlockSpec((tm, GLA_V), lambda i, j: (i, 0)),
                  pl.BlockSpec((ATTN_Q + GLA_V, tn), lambda i, j: (0, j)),
                  pl.BlockSpec((tm, tn), lambda i, j: (i, j)),
                  pl.BlockSpec((None, 1, tn), lambda i, j: ((i * tm) // SEQ * N_MOD + 2, 0, j))],
        out_specs=pl.BlockSpec((tm, tn), lambda i, j: (i, j)),
        out_shape=jax.ShapeDtypeStruct((NTOK, D_MODEL), F32),
        compiler_params=_params("arbitrary", "arbitrary"),
        name="outproj",
    )(attn, gla, w_out, x2, modr)


ROUTE_TR = 256
LANE = 128
NOT_TOP = 99.0
CAND_ROWS = 8 * (PEER_TOPK + 1)


def _topk_ranked(s):
    t = s.shape[1]
    rows = lax.broadcasted_iota(jnp.int32, s.shape, 0).astype(F32)
    work = s
    rank = jnp.full(s.shape, NOT_TOP, F32)
    vals = []
    for r in range(PEER_TOPK):
        m = jnp.max(work, axis=0, keepdims=True)
        idx = jnp.min(jnp.where(work == m, rows, float(N_KEYS)), axis=0, keepdims=True)
        sel = rows == idx
        rank = jnp.where(sel, float(r), rank)
        work = jnp.where(sel, -jnp.inf, work)
        vals.append(m)
    return vals, rank


def _route_kernel(q_ref, keys_ref, k2_ref, e2_ref, n1_ref, e1_ref):
    q = q_ref[...].astype(BF)
    s1 = lax.dot_general(keys_ref[0], q[:, :N_KEYS], NT_DIMS, preferred_element_type=F32)
    s2 = lax.dot_general(keys_ref[1], q[:, N_KEYS:], NT_DIMS, preferred_element_type=F32)
    v1, rank1 = _topk_ranked(s1)
    v2, rank2 = _topk_ranked(s2)
    t = s1.shape[1]
    t2a = jnp.concatenate(v2[:8], axis=0)
    t2b = jnp.concatenate(v2[8:], axis=0)
    cand = jnp.concatenate([v1[0] + t2a, v1[0] + t2b] + [v1[r] + t2a for r in range(1, PEER_TOPK)], axis=0)
    i = lax.broadcasted_iota(jnp.int32, (CAND_ROWS, t), 0)
    flat = jnp.where(i < 16, i, 2 * i - (i & 7) - 16).astype(F32)
    work = cand
    for _ in range(PEER_TOPK):
        m = jnp.max(work, axis=0, keepdims=True)
        idx = jnp.min(jnp.where(work == m, flat, 1e9), axis=0, keepdims=True)
        work = jnp.where(flat == idx, -jnp.inf, work)
    chosen = work == -jnp.inf
    top = v1[0] + v2[0]
    z = jnp.sum(jnp.where(chosen, jnp.exp(cand - top), 0.0), axis=0, keepdims=True)
    cnt = jnp.where(chosen, 1.0, 0.0)
    n = [jnp.sum(cnt[0:16], axis=0, keepdims=True)]
    n += [jnp.sum(cnt[8 * (r + 1):8 * (r + 2)], axis=0, keepdims=True) for r in range(1, PEER_TOPK)]
    n1 = jnp.zeros_like(s1)
    for r in range(PEER_TOPK):
        n1 = jnp.where(rank1 == float(r), n[r], n1)
    k2_ref[...] = rank2.astype(k2_ref.dtype)
    e2_ref[...] = jnp.exp(s2 - v2[0]).astype(e2_ref.dtype)
    e1 = jnp.exp(s1 - v1[0]) / z
    for c in range(t // LANE):
        n1_ref[c] = n1[:, c * LANE:(c + 1) * LANE]
        e1_ref[c] = e1[:, c * LANE:(c + 1) * LANE]


def _route(q, keys):
    tr = ROUTE_TR
    shp_bf = jax.ShapeDtypeStruct((PEER_HEADS, N_KEYS, NTOK), BF)
    ospec = pl.BlockSpec((None, N_KEYS, tr), lambda i, h: (h, 0, i))
    shp_a = jax.ShapeDtypeStruct((PEER_HEADS, NTOK // LANE, N_KEYS, LANE), F32)
    aspec = pl.BlockSpec((None, tr // LANE, N_KEYS, LANE), lambda i, h: (h, i, 0, 0))
    return pl.pallas_call(
        _route_kernel,
        grid=(NTOK // tr, PEER_HEADS),
        in_specs=[pl.BlockSpec((tr, PEER_DKEY), lambda i, h: (i, h)),
                  pl.BlockSpec((None, 2, N_KEYS, PEER_DKEY // 2), lambda i, h: (h, 0, 0, 0))],
        out_specs=[ospec, ospec, aspec, aspec],
        out_shape=[shp_bf, shp_bf, shp_a, shp_a],
        compiler_params=_params("arbitrary", "arbitrary"),
        name="peer_route",
    )(q, keys)


PEER_TM = 512
PEER_TE = 512
PEER_NJ = N_EXPERTS // PEER_TE
PEER_STEPS = (NTOK // PEER_TM) * PEER_NJ


def _peer_stages(h_ref, u_ref, v_ref, k2_ref, e2_ref, n1_ref, e1_ref, o_ref, act_w, act_r, wt_w, wt_r, jb):
    half_e, half_t = PEER_TE // 2, PEER_TM // 2
    rows_c = D_MODEL // 8
    for k in range(4):
        e0, t0 = half_e * (k // 2), half_t * (k % 2)
        act_w[e0:e0 + half_e, t0:t0 + half_t] = lax.dot_general(
            u_ref[e0:e0 + half_e, :], h_ref[t0:t0 + half_t, :], NT_DIMS, preferred_element_type=F32)

        a = jb * (PEER_TE // N_KEYS) + k
        es = slice(k * N_KEYS, (k + 1) * N_KEYS)
        for lc in range(PEER_TM // LANE):
            ls = slice(lc * LANE, (lc + 1) * LANE)
            g = None
            for h in range(PEER_HEADS):
                n1 = n1_ref[h, lc, pl.ds(a, 1), :].astype(BF)
                e1 = e1_ref[h, lc, pl.ds(a, 1), :].astype(BF)
                t = jnp.where(k2_ref[h, :, ls] < n1, e2_ref[h, :, ls], jnp.zeros((), BF)) * e1
                g = t if g is None else g + t
            x = act_r[es, ls]
            ge = 0.5 * x * (1.0 + lax.erf(x * (2.0 ** -0.5)))
            wt_w[es, ls] = g * ge.astype(BF)

        for c in (2 * k, 2 * k + 1):
            rs = slice(c * rows_c, (c + 1) * rows_c)
            o_ref[rs, :] += jnp.dot(v_ref[rs, :], wt_r[...], preferred_element_type=F32)


def _peer_kernel(h_ref, u_ref, v_ref, k2_ref, e2_ref, n1_ref, e1_ref, o_ref, act0, act1, wt0, wt1):
    s = pl.program_id(0)
    jb = jnp.clip(s - 1, 0, PEER_STEPS - 1) % PEER_NJ

    @pl.when(s == 0)
    def _():
        act1[...] = jnp.zeros_like(act1)
        wt1[...] = jnp.zeros_like(wt1)

    @pl.when(jnp.logical_or(s <= 2, (s - 2) % PEER_NJ == 0))
    def _():
        o_ref[...] = jnp.zeros_like(o_ref)

    refs = (h_ref, u_ref, v_ref, k2_ref, e2_ref, n1_ref, e1_ref, o_ref)

    @pl.when(s % 2 == 0)
    def _():
        _peer_stages(*refs, act0, act1, wt0, wt1, jb)

    @pl.when(s % 2 == 1)
    def _():
        _peer_stages(*refs, act1, act0, wt1, wt0, jb)


def _peer(h2, u, vt, k2, e2, n1, e1):
    tm, te, nj, last = PEER_TM, PEER_TE, PEER_NJ, PEER_STEPS - 1

    def sa(s):
        return jnp.minimum(s, last)

    def sb(s):
        return jnp.clip(s - 1, 0, last)

    def sc(s):
        return jnp.clip(s - 2, 0, last)

    rspec = pl.BlockSpec((PEER_HEADS, N_KEYS, tm), lambda s: (0, 0, sb(s) // nj), pipeline_mode=pl.Buffered(1))
    aspec = pl.BlockSpec((PEER_HEADS, tm // LANE, N_KEYS, LANE), lambda s: (0, sb(s) // nj, 0, 0),
                         pipeline_mode=pl.Buffered(1))
    return pl.pallas_call(
        _peer_kernel,
        grid=(PEER_STEPS + 2,),
        in_specs=[pl.BlockSpec((tm, D_MODEL), lambda s: (sa(s) // nj, 0), pipeline_mode=pl.Buffered(1)),
                  pl.BlockSpec((te, D_MODEL), lambda s: (sa(s) % nj, 0)),
                  pl.BlockSpec((D_MODEL, te), lambda s: (0, sc(s) % nj)),
                  rspec, rspec, aspec, aspec],
        out_specs=pl.BlockSpec((D_MODEL, tm), lambda s: (0, sc(s) // nj)),
        out_shape=jax.ShapeDtypeStruct((D_MODEL, NTOK), F32),
        scratch_shapes=[pltpu.VMEM((te, tm), F32), pltpu.VMEM((te, tm), F32),
                        pltpu.VMEM((te, tm), BF), pltpu.VMEM((te, tm), BF)],
        compiler_params=_params("arbitrary"),
        name="peer_dense",
    )(h2, u, vt, k2, e2, n1, e1)


def _final_kernel(x_ref, pt_ref, gate_ref, g_ref, o_ref):
    y = x_ref[...] + gate_ref[...] * pt_ref[...].T
    ms = jnp.mean(y * y, axis=-1, keepdims=True)
    o_ref[...] = y * lax.rsqrt(ms + EPS) * g_ref[...]


def _final(x1, peer_t, modr, final_g):
    tr = 256
    return pl.pallas_call(
        _final_kernel,
        grid=(NTOK // tr,),
        in_specs=[pl.BlockSpec((tr, D_MODEL), lambda i: (i, 0)),
                  pl.BlockSpec((D_MODEL, tr), lambda i: (0, i)),
                  pl.BlockSpec((None, 1, D_MODEL), lambda i: ((i * tr) // SEQ * N_MOD + 5, 0, 0)),
                  pl.BlockSpec((1, D_MODEL), lambda i: (0, 0))],
        out_specs=pl.BlockSpec((tr, D_MODEL), lambda i: (i, 0)),
        out_shape=jax.ShapeDtypeStruct((NTOK, D_MODEL), F32),
        compiler_params=_params("arbitrary"),
        name="final_norm",
    )(x1, peer_t, modr, final_g)


def kernel(x, c, w_ada, b_ada, norm1_g, norm2_g, w_in, attn_sinks, rel_bias, gla_w_gk2, gla_b_gk2,
           gla_norm_g, w_out, peer_w_q, peer_keys, peer_u, peer_v, final_g):
    assert x.shape == (BATCH, SEQ, D_MODEL) and w_ada.shape[0] == 1
    x2 = x.reshape(NTOK, D_MODEL)
    c8 = jnp.pad(c, ((0, 8 - BATCH), (0, 0)))
    mod = _ada(c8, w_ada[0], b_ada[0][None, :])
    modr = mod[:BATCH].reshape(BATCH * N_MOD, 1, D_MODEL)

    wi = w_in[0]
    o = np.cumsum([0, ATTN_Q, ATTN_KV, ATTN_KV, GLA_K, GLA_K, GLA_V, GLA_RANK, GLA_V])
    aq, ak, av, gq, gk, gv, glo, gout = [wi[:, o[i]:o[i + 1]] for i in range(8)]
    w_main = jnp.concatenate([aq, gq, gk, gv, gout, ak, av], axis=1).astype(BF)
    w_glow = jnp.pad(glo, ((0, 0), (0, 128 - GLA_RANK))).astype(BF)
    w2p = jnp.pad(gla_w_gk2[0], ((0, 128 - GLA_RANK), (0, 0)))

    h1 = _norm_mod(x2, norm1_g[0][None, :], modr, 1, 0)
    proj = _matmul(h1, w_main, BF, 1024, 512, "in_proj")
    glow = _matmul(h1, w_glow, F32, 1024, 128, "glow_proj")

    bias_tab = _bias_table(rel_bias, attn_sinks[0])
    attn = _attention(proj, bias_tab)
    gla = _gla(proj, glow, w2p, gla_b_gk2[0][None, :], gla_norm_g[0][None, :])
    x1 = _outproj(attn, gla, w_out[0].astype(BF), x2, modr)

    h2 = _norm_mod(x1, norm2_g[0][None, :], modr, 4, 3)
    q = _matmul(h2, peer_w_q[0].astype(BF), F32, 1024, 512, "peer_q")
    k2, e2, n1, e1 = _route(q, peer_keys[0].astype(BF))
    peer_t = _peer(h2, peer_u[0].astype(BF), peer_v[0].T.astype(BF), k2, e2, n1, e1)
    out = _final(x1, peer_t, modr, final_g[None, :])
    return out.reshape(BATCH, SEQ, D_MODEL)
```

```python
import math

import jax
import jax.numpy as jnp
from jax import lax
from jax.experimental import pallas as pl
from jax.experimental.pallas import tpu as pltpu

BF = jnp.bfloat16
F32 = jnp.float32

D_MODEL = 4096
BATCH = 4
SEQ = 2048
NTOK = BATCH * SEQ
HEAD_DIM = 128
ATTN_HEADS = 16
ATTN_KV_HEADS = 2
ATTN_GROUP = 8
WINDOW = 128
BLOCK = 128
N_BUCKETS = 32
MAX_DISTANCE = 128
GLA_HEADS = 4
GLA_DV = 512
GLA_DK = 256
GLA_RANK = 16
GLA_TAU = 16.0
GLA_CHUNK = 64
ATTN_Q = 2048
ATTN_KV = 256
GLA_K = 1024
GLA_V = 2048
PEER_HEADS = 8
N_KEYS = 128
N_EXPERTS = N_KEYS * N_KEYS
PEER_TOPK = 16
PEER_DKEY = 256
N_MOD = 6
EPS = 1e-6
NEG_INF = -1e30

OFF_AQ, OFF_AK, OFF_AV, OFF_GQ, OFF_GK, OFF_GV, OFF_GLOW, OFF_GOUT = 0, 2048, 2304, 2560, 3584, 4608, 6656, 6672
HALF_K = GLA_K // 2

VMEM_LIMIT = 56 * 1024 * 1024

NT_DIMS = (((1,), (1,)), ((), ()))
TN_DIMS = (((0,), (0,)), ((), ()))


def _params(*sem, flags=None):
    return pltpu.CompilerParams(dimension_semantics=sem, vmem_limit_bytes=VMEM_LIMIT, flags=flags)


def _ada_kernel(c_ref, w_ref, b_ref, o_ref):
    c = c_ref[...]
    ca = c / (1.0 + jnp.exp(-c))
    o_ref[...] = jnp.dot(ca.astype(BF), w_ref[...].astype(BF), preferred_element_type=F32) + b_ref[...]


def _ada(c8, w_ada, b_ada):
    tn = 512
    n = N_MOD * D_MODEL
    return pl.pallas_call(
        _ada_kernel,
        grid=(n // tn,),
        in_specs=[pl.BlockSpec((8, D_MODEL), lambda j: (0, 0)),
                  pl.BlockSpec((D_MODEL, tn), lambda j: (0, j)),
                  pl.BlockSpec((1, tn), lambda j: (0, j))],
        out_specs=pl.BlockSpec((8, tn), lambda j: (0, j)),
        out_shape=jax.ShapeDtypeStruct((8, n), F32),
        compiler_params=_params("arbitrary"),
        name="ada",
    )(c8, w_ada, b_ada)


def _norm_mod_kernel(x_ref, g_ref, sc_ref, sh_ref, o_ref):
    x = x_ref[...]
    ms = jnp.mean(x * x, axis=-1, keepdims=True)
    y = x * lax.rsqrt(ms + EPS) * g_ref[...]
    o_ref[...] = (y * (1.0 + sc_ref[...]) + sh_ref[...]).astype(o_ref.dtype)


def _norm_mod(x2, g, modr, scale_idx, shift_idx):
    tr = 256
    return pl.pallas_call(
        _norm_mod_kernel,
        grid=(NTOK // tr,),
        in_specs=[pl.BlockSpec((tr, D_MODEL), lambda i: (i, 0)),
                  pl.BlockSpec((1, D_MODEL), lambda i: (0, 0)),
                  pl.BlockSpec((None, 1, D_MODEL), lambda i: ((i * tr) // SEQ * N_MOD + scale_idx, 0, 0)),
                  pl.BlockSpec((None, 1, D_MODEL), lambda i: ((i * tr) // SEQ * N_MOD + shift_idx, 0, 0))],
        out_specs=pl.BlockSpec((tr, D_MODEL), lambda i: (i, 0)),
        out_shape=jax.ShapeDtypeStruct((NTOK, D_MODEL), BF),
        compiler_params=_params("arbitrary"),
        name="norm_mod",
    )(x2, g, modr, modr)


def _mm_kernel(a_ref, w_ref, o_ref, wbf_ref):
    @pl.when(pl.program_id(1) == 0)
    def _():
        wbf_ref[...] = w_ref[...].astype(BF)

    o_ref[...] = jnp.dot(a_ref[...], wbf_ref[...], preferred_element_type=F32).astype(o_ref.dtype)


def _matmul(a, w, ncols, out_dtype, tm, tn, name):
    m, k = a.shape
    return pl.pallas_call(
        _mm_kernel,
        grid=(ncols // tn, m // tm),
        in_specs=[pl.BlockSpec((tm, k), lambda j, i: (i, 0)),
                  pl.BlockSpec((k, tn), lambda j, i: (0, j))],
        out_specs=pl.BlockSpec((tm, tn), lambda j, i: (i, j)),
        out_shape=jax.ShapeDtypeStruct((m, ncols), out_dtype),
        scratch_shapes=[pltpu.VMEM((k, tn), BF)],
        compiler_params=_params("arbitrary", "arbitrary"),
        name=name,
    )(a, w)


def _t5_bucket(dist):
    max_exact = N_BUCKETS // 2
    d = jnp.maximum(dist, 0)
    log_ratio = jnp.log(jnp.maximum(d, 1).astype(F32) / max_exact) / math.log(MAX_DISTANCE / max_exact)
    large = max_exact + (log_ratio * (N_BUCKETS - max_exact)).astype(jnp.int32)
    large = jnp.minimum(large, N_BUCKETS - 1)
    return jnp.where(d < max_exact, d, large)


def _bias_kernel(rel_ref, sink_ref, bucket_ref, o_ref):
    h = pl.program_id(0)
    bucket = bucket_ref[...]
    acc = jnp.full((BLOCK, 2 * BLOCK), NEG_INF, F32)
    for b in range(N_BUCKETS):
        acc = jnp.where(bucket == b, rel_ref[b, h], acc)
    col = lax.broadcasted_iota(jnp.int32, (BLOCK, 2 * BLOCK), 1)
    acc = jnp.where(col == 0, sink_ref[h], acc)
    o_ref[0, 0] = acc
    o_ref[1, 0] = jnp.where(jnp.logical_and(col < BLOCK, col > 0), NEG_INF, acc)


def _bias_table(rel_bias, sinks):
    qi = jnp.arange(BLOCK)[:, None]
    kj = jnp.arange(2 * BLOCK)[None, :]
    dist = qi + BLOCK - kj
    in_window = (dist >= 0) & (dist < WINDOW)
    bucket = jnp.where(in_window, _t5_bucket(dist), -1).astype(jnp.int32)
    tab = pl.pallas_call(
        _bias_kernel,
        grid=(ATTN_HEADS,),
        in_specs=[pl.BlockSpec(memory_space=pltpu.SMEM),
                  pl.BlockSpec(memory_space=pltpu.SMEM),
                  pl.BlockSpec((BLOCK, 2 * BLOCK), lambda h: (0, 0))],
        out_specs=pl.BlockSpec((2, 1, BLOCK, 2 * BLOCK), lambda h: (0, h, 0, 0)),
        out_shape=jax.ShapeDtypeStruct((2, ATTN_HEADS, BLOCK, 2 * BLOCK), F32),
        compiler_params=_params("arbitrary"),
        name="bias_table",
    )(rel_bias, sinks, bucket)
    return tab.reshape(2, ATTN_KV_HEADS, ATTN_GROUP * BLOCK, 2 * BLOCK)


def _attn_kernel(q_ref, kp_ref, kc_ref, vp_ref, vc_ref, bias_ref, o_ref):
    q = q_ref[...]
    qs = jnp.concatenate([q[:, g * HEAD_DIM:(g + 1) * HEAD_DIM] for g in range(ATTN_GROUP)], axis=0)
    row = lax.broadcasted_iota(jnp.int32, (BLOCK, HEAD_DIM), 0)
    kp = jnp.where(row == 0, 0.0, kp_ref[...].astype(F32)).astype(BF)
    vp = jnp.where(row == 0, 0.0, vp_ref[...].astype(F32)).astype(BF)
    kcat = jnp.concatenate([kp, kc_ref[...]], axis=0)
    vcat = jnp.concatenate([vp, vc_ref[...]], axis=0)
    s = lax.dot_general(qs, kcat, NT_DIMS, preferred_element_type=F32) * (HEAD_DIM ** -0.5) + bias_ref[...]
    m = jnp.max(s, axis=-1, keepdims=True)
    p = jnp.exp(s - m).astype(BF)
    vaug = jnp.concatenate([vcat, jnp.ones((2 * BLOCK, HEAD_DIM), BF)], axis=1)
    oa = jnp.dot(p, vaug, preferred_element_type=F32)
    o = oa[:, :HEAD_DIM] / oa[:, HEAD_DIM:]
    o_ref[...] = jnp.concatenate([o[g * BLOCK:(g + 1) * BLOCK, :] for g in range(ATTN_GROUP)], axis=1).astype(o_ref.dtype)


def _attention(proj, bias_tab):
    nb = SEQ // BLOCK
    gw = ATTN_GROUP * HEAD_DIM
    kblk = OFF_AK // HEAD_DIM
    vblk = OFF_AV // HEAD_DIM
    return pl.pallas_call(
        _attn_kernel,
        grid=(ATTN_KV_HEADS, BATCH, nb),
        in_specs=[pl.BlockSpec((BLOCK, gw), lambda h, b, n: (b * nb + n, h)),
                  pl.BlockSpec((BLOCK, HEAD_DIM), lambda h, b, n: (b * nb + jnp.maximum(n - 1, 0), kblk + h)),
                  pl.BlockSpec((BLOCK, HEAD_DIM), lambda h, b, n: (b * nb + n, kblk + h)),
                  pl.BlockSpec((BLOCK, HEAD_DIM), lambda h, b, n: (b * nb + jnp.maximum(n - 1, 0), vblk + h)),
                  pl.BlockSpec((BLOCK, HEAD_DIM), lambda h, b, n: (b * nb + n, vblk + h)),
                  pl.BlockSpec((None, None, gw, 2 * BLOCK), lambda h, b, n: (jnp.where(n == 0, 1, 0), h, 0, 0))],
        out_specs=pl.BlockSpec((BLOCK, gw), lambda h, b, n: (b * nb + n, h)),
        out_shape=jax.ShapeDtypeStruct((NTOK, ATTN_Q), BF),
        compiler_params=_params("arbitrary", "arbitrary", "arbitrary"),
        name="swa_attn",
    )(proj, proj, proj, proj, proj, bias_tab)


GLA_TB = 128


def _gla_kernel(q0_ref, q1_ref, k0_ref, k1_ref, v0_ref, v1_ref, v2_ref, v3_ref, go_ref, glow_ref, w2_ref, b2_ref,
                g_ref, o_ref, st_ref):
    q_refs, k_refs, v_refs = (q0_ref, q1_ref), (k0_ref, k1_ref), (v0_ref, v1_ref, v2_ref, v3_ref)
    @pl.when(pl.program_id(1) == 0)
    def _():
        st_ref[...] = jnp.zeros_like(st_ref)

    C = GLA_CHUNK
    ri = lax.broadcasted_iota(jnp.int32, (C, C), 0)
    ci = lax.broadcasted_iota(jnp.int32, (C, C), 1)
    tril = ri >= ci
    trilf = tril.astype(F32)
    for c in range(GLA_TB // C):
        rs = slice(c * C, (c + 1) * C)
        z = jnp.dot(glow_ref[rs, :], w2_ref[...], precision=lax.Precision.HIGHEST,
                    preferred_element_type=F32) + b2_ref[...]
        la = (jnp.minimum(z, 0.0) - jnp.log1p(jnp.exp(-jnp.abs(z)))) * (1.0 / GLA_TAU)
        bcum = jnp.dot(trilf, la, precision=lax.Precision.HIGHEST, preferred_element_type=F32)
        blast = bcum[C - 1:C, :]
        eb = jnp.exp(bcum)
        enb = jnp.exp(-bcum)
        edl = jnp.exp(blast - bcum)
        el = jnp.exp(blast)
        for h in range(GLA_HEADS):
            ks = slice(h * GLA_DK, (h + 1) * GLA_DK)
            vs = slice(h * GLA_DV, (h + 1) * GLA_DV)
            hs = slice((h % 2) * GLA_DK, (h % 2 + 1) * GLA_DK)
            q = q_refs[h // 2][rs, hs].astype(F32) * (GLA_DK ** -0.5)
            k = k_refs[h // 2][rs, hs].astype(F32)
            v = v_refs[h][rs, :]
            qd = (q * eb[:, ks]).astype(BF)
            ki = (k * enb[:, ks]).astype(BF)
            kd = (k * edl[:, ks]).astype(BF)
            a = lax.dot_general(qd, ki, NT_DIMS, preferred_element_type=F32)
            a = jnp.where(tril, a, 0.0).astype(BF)
            st = st_ref[h]
            o = (jnp.dot(a, v, preferred_element_type=F32)
                 + lax.dot_general(qd, st.astype(BF), NT_DIMS, preferred_element_type=F32))
            st_ref[h] = st * el[:, ks] + lax.dot_general(v, kd, TN_DIMS, preferred_element_type=F32)
            ms = jnp.mean(o * o, axis=-1, keepdims=True)
            y = o * lax.rsqrt(ms + EPS) * g_ref[...]
            gate = go_ref[rs, vs].astype(F32)
            y = y * (gate / (1.0 + jnp.exp(-gate)))
            o_ref[rs, vs] = y.astype(o_ref.dtype)


def _gla(proj, gout, glow, w2p, b2, gnorm):
    nt = SEQ // GLA_TB

    def col(width, off):
        return pl.BlockSpec((GLA_TB, width), lambda b, t: (b * nt + t, off // width))

    return pl.pallas_call(
        _gla_kernel,
        grid=(BATCH, nt),
        in_specs=[col(HALF_K, OFF_GQ), col(HALF_K, OFF_GQ + HALF_K), col(HALF_K, OFF_GK), col(HALF_K, OFF_GK + HALF_K),
                  col(GLA_DV, OFF_GV), col(GLA_DV, OFF_GV + GLA_DV), col(GLA_DV, OFF_GV + 2 * GLA_DV),
                  col(GLA_DV, OFF_GV + 3 * GLA_DV),
                  pl.BlockSpec((GLA_TB, GLA_V), lambda b, t: (b * nt + t, 0)),
                  pl.BlockSpec((GLA_TB, 128), lambda b, t: (b * nt + t, 0)),
                  pl.BlockSpec((128, GLA_K), lambda b, t: (0, 0)),
                  pl.BlockSpec((1, GLA_K), lambda b, t: (0, 0)),
                  pl.BlockSpec((1, GLA_DV), lambda b, t: (0, 0))],
        out_specs=pl.BlockSpec((GLA_TB, GLA_V), lambda b, t: (b * nt + t, 0)),
        out_shape=jax.ShapeDtypeStruct((NTOK, GLA_V), BF),
        scratch_shapes=[pltpu.VMEM((GLA_HEADS, GLA_DV, GLA_DK), F32)],
        compiler_params=_params("arbitrary", "arbitrary"),
        name="gla",
    )(proj, proj, proj, proj, proj, proj, proj, proj, gout, glow, w2p, b2, gnorm)


def _outproj_kernel(a_ref, g_ref, w_ref, x_ref, gate_ref, o_ref, wbf_ref):
    @pl.when(pl.program_id(1) == 0)
    def _():
        wbf_ref[...] = w_ref[...].astype(BF)

    acc = (jnp.dot(a_ref[...], wbf_ref[0:ATTN_Q, :], preferred_element_type=F32)
           + jnp.dot(g_ref[...], wbf_ref[ATTN_Q:ATTN_Q + GLA_V, :], preferred_element_type=F32))
    o_ref[...] = x_ref[...] + gate_ref[...] * acc


def _outproj(attn, gla, w_out, x2, modr):
    tm, tn = 1024, 512
    return pl.pallas_call(
        _outproj_kernel,
        grid=(D_MODEL // tn, NTOK // tm),
        in_specs=[pl.BlockSpec((tm, ATTN_Q), lambda j, i: (i, 0)),
                  pl.BlockSpec((tm, GLA_V), lambda j, i: (i, 0)),
                  pl.BlockSpec((ATTN_Q + GLA_V, tn), lambda j, i: (0, j)),
                  pl.BlockSpec((tm, tn), lambda j, i: (i, j)),
                  pl.BlockSpec((None, 1, tn), lambda j, i: ((i * tm) // SEQ * N_MOD + 2, 0, j))],
        out_specs=pl.BlockSpec((tm, tn), lambda j, i: (i, j)),
        out_shape=jax.ShapeDtypeStruct((NTOK, D_MODEL), F32),
        scratch_shapes=[pltpu.VMEM((ATTN_Q + GLA_V, tn), BF)],
        compiler_params=_params("arbitrary", "arbitrary"),
        name="outproj",
    )(attn, gla, w_out, x2, modr)


ROUTE_TR = 256
LANE = 128
NOT_TOP = 99.0
CAND_ROWS = 8 * (PEER_TOPK + 1)


def _extract_top(work, keys, exact):
    for _ in range(PEER_TOPK):
        m = jnp.max(work, axis=0, keepdims=True)
        sel = work == m
        if exact:
            first = jnp.min(jnp.where(sel, keys, 1e9), axis=0, keepdims=True)
            sel = keys == first
        work = jnp.where(sel, -jnp.inf, work)
        yield m, sel, work


def _route_compute(s1, s2, exact):
    t = s1.shape[1]
    s = jnp.concatenate([s1, s2], axis=1)
    rows = lax.broadcasted_iota(jnp.int32, s.shape, 0).astype(F32)
    rank = jnp.full(s.shape, NOT_TOP, F32)
    vals = []
    work = s
    for r, (m, sel, work) in enumerate(_extract_top(s, rows, exact)):
        rank = jnp.where(sel, float(r), rank)
        vals.append(m)
    removed = jnp.sum(jnp.where(work == -jnp.inf, 1.0, 0.0), axis=0, keepdims=True)
    bad = jnp.max(jnp.abs(removed - PEER_TOPK))
    rank1, rank2 = rank[:, :t], rank[:, t:]
    v1 = [v[:, :t] for v in vals]
    v2 = [v[:, t:] for v in vals]
    t2a = jnp.concatenate(v2[:8], axis=0)
    t2b = jnp.concatenate(v2[8:], axis=0)
    cand = jnp.concatenate([v1[0] + t2a, v1[0] + t2b] + [v1[r] + t2a for r in range(1, PEER_TOPK)], axis=0)
    i = lax.broadcasted_iota(jnp.int32, (CAND_ROWS, t), 0)
    flat = jnp.where(i < 16, i, 2 * i - (i & 7) - 16).astype(F32)
    work = cand
    for _, _, work in _extract_top(cand, flat, exact):
        pass
    chosen = work == -jnp.inf
    cnt = jnp.where(chosen, 1.0, 0.0)
    bad = jnp.maximum(bad, jnp.max(jnp.abs(jnp.sum(cnt, axis=0, keepdims=True) - PEER_TOPK)))
    top = v1[0] + v2[0]
    z = jnp.sum(jnp.where(chosen, jnp.exp(cand - top), 0.0), axis=0, keepdims=True)
    n = [jnp.sum(cnt[0:16], axis=0, keepdims=True)]
    n += [jnp.sum(cnt[8 * (r + 1):8 * (r + 2)], axis=0, keepdims=True) for r in range(1, PEER_TOPK)]
    n1 = jnp.zeros_like(s1)
    for r in range(PEER_TOPK):
        n1 = jnp.where(rank1 == float(r), n[r], n1)
    e2 = jnp.exp(s2 - v2[0])
    e1 = jnp.exp(s1 - v1[0]) / z
    return rank2, e2, n1, e1, bad


def _route_kernel(q_ref, keys_ref, k2_ref, e2_ref, n1_ref, e1_ref):
    q = q_ref[...].astype(BF)
    s1 = lax.dot_general(keys_ref[0], q[:, :N_KEYS], NT_DIMS, preferred_element_type=F32)
    s2 = lax.dot_general(keys_ref[1], q[:, N_KEYS:], NT_DIMS, preferred_element_type=F32)

    def emit(rank2, e2, n1, e1):
        k2_ref[...] = rank2.astype(k2_ref.dtype)
        e2_ref[...] = e2.astype(e2_ref.dtype)
        for c in range(s1.shape[1] // LANE):
            n1_ref[c] = n1[:, c * LANE:(c + 1) * LANE]
            e1_ref[c] = e1[:, c * LANE:(c + 1) * LANE]

    *fast, bad = _route_compute(s1, s2, exact=False)
    emit(*fast)

    @pl.when(bad > 0.0)
    def _():
        *slow, _ = _route_compute(s1, s2, exact=True)
        emit(*slow)


def _route(q, keys):
    tr = ROUTE_TR
    shp_bf = jax.ShapeDtypeStruct((PEER_HEADS, N_KEYS, NTOK), BF)
    ospec = pl.BlockSpec((None, N_KEYS, tr), lambda i, h: (h, 0, i))
    shp_a = jax.ShapeDtypeStruct((PEER_HEADS, NTOK // LANE, N_KEYS, LANE), F32)
    aspec = pl.BlockSpec((None, tr // LANE, N_KEYS, LANE), lambda i, h: (h, i, 0, 0))
    return pl.pallas_call(
        _route_kernel,
        grid=(NTOK // tr, PEER_HEADS),
        in_specs=[pl.BlockSpec((tr, PEER_DKEY), lambda i, h: (i, h)),
                  pl.BlockSpec((None, 2, N_KEYS, PEER_DKEY // 2), lambda i, h: (h, 0, 0, 0))],
        out_specs=[ospec, ospec, aspec, aspec],
        out_shape=[shp_bf, shp_bf, shp_a, shp_a],
        compiler_params=_params("arbitrary", "arbitrary"),
        name="peer_route",
    )(q, keys)


PEER_TM = 512
PEER_TE = 512
PEER_NJ = N_EXPERTS // PEER_TE
PEER_STEPS = (NTOK // PEER_TM) * PEER_NJ


def _peer_stages(h_ref, u_ref, v_ref, k2_ref, e2_ref, n1_ref, e1_ref, o_ref, act_w, act_r, wt_w, wt_r, jb):
    half_e, half_t = PEER_TE // 2, PEER_TM // 2
    rows_c = D_MODEL // 8
    for k in range(4):
        e0, t0 = half_e * (k // 2), half_t * (k % 2)
        act_w[e0:e0 + half_e, t0:t0 + half_t] = lax.dot_general(
            u_ref[e0:e0 + half_e, :], h_ref[t0:t0 + half_t, :], NT_DIMS, preferred_element_type=F32)

        a = jb * (PEER_TE // N_KEYS) + k
        es = slice(k * N_KEYS, (k + 1) * N_KEYS)
        for lc in range(PEER_TM // LANE):
            ls = slice(lc * LANE, (lc + 1) * LANE)
            g = None
            for h in range(PEER_HEADS):
                n1 = n1_ref[h, lc, pl.ds(a, 1), :].astype(BF)
                e1 = e1_ref[h, lc, pl.ds(a, 1), :].astype(BF)
                t = jnp.where(k2_ref[h, :, ls] < n1, e2_ref[h, :, ls], jnp.zeros((), BF)) * e1
                g = t if g is None else g + t
            x = act_r[es, ls]
            ge = 0.5 * x * (1.0 + lax.erf(x * (2.0 ** -0.5)))
            wt_w[es, ls] = g * ge.astype(BF)

        for c in (2 * k, 2 * k + 1):
            rs = slice(c * rows_c, (c + 1) * rows_c)
            o_ref[rs, :] += jnp.dot(v_ref[rs, :], wt_r[...], preferred_element_type=F32)


def _peer_kernel(h_ref, u_ref, v_ref, k2_ref, e2_ref, n1_ref, e1_ref, o_ref, act0, act1, wt0, wt1):
    s = pl.program_id(0)
    jb = jnp.clip(s - 1, 0, PEER_STEPS - 1) % PEER_NJ

    @pl.when(s == 0)
    def _():
        act1[...] = jnp.zeros_like(act1)
        wt1[...] = jnp.zeros_like(wt1)

    @pl.when(jnp.logical_or(s <= 2, (s - 2) % PEER_NJ == 0))
    def _():
        o_ref[...] = jnp.zeros_like(o_ref)

    refs = (h_ref, u_ref, v_ref, k2_ref, e2_ref, n1_ref, e1_ref, o_ref)

    @pl.when(s % 2 == 0)
    def _():
        _peer_stages(*refs, act0, act1, wt0, wt1, jb)

    @pl.when(s % 2 == 1)
    def _():
        _peer_stages(*refs, act1, act0, wt1, wt0, jb)


def _peer(h2, u, vt, k2, e2, n1, e1):
    tm, te, nj, last = PEER_TM, PEER_TE, PEER_NJ, PEER_STEPS - 1

    def sa(s):
        return jnp.minimum(s, last)

    def sb(s):
        return jnp.clip(s - 1, 0, last)

    def sc(s):
        return jnp.clip(s - 2, 0, last)

    rspec = pl.BlockSpec((PEER_HEADS, N_KEYS, tm), lambda s: (0, 0, sb(s) // nj), pipeline_mode=pl.Buffered(1))
    aspec = pl.BlockSpec((PEER_HEADS, tm // LANE, N_KEYS, LANE), lambda s: (0, sb(s) // nj, 0, 0),
                         pipeline_mode=pl.Buffered(1))
    return pl.pallas_call(
        _peer_kernel,
        grid=(PEER_STEPS + 2,),
        in_specs=[pl.BlockSpec((tm, D_MODEL), lambda s: (sa(s) // nj, 0), pipeline_mode=pl.Buffered(1)),
                  pl.BlockSpec((te, D_MODEL), lambda s: (sa(s) % nj, 0)),
                  pl.BlockSpec((D_MODEL, te), lambda s: (0, sc(s) % nj)),
                  rspec, rspec, aspec, aspec],
        out_specs=pl.BlockSpec((D_MODEL, tm), lambda s: (0, sc(s) // nj)),
        out_shape=jax.ShapeDtypeStruct((D_MODEL, NTOK), F32),
        scratch_shapes=[pltpu.VMEM((te, tm), F32), pltpu.VMEM((te, tm), F32),
                        pltpu.VMEM((te, tm), BF), pltpu.VMEM((te, tm), BF)],
        compiler_params=_params("arbitrary"),
        name="peer_dense",
    )(h2, u, vt, k2, e2, n1, e1)


def _final_kernel(x_ref, pt_ref, gate_ref, g_ref, o_ref):
    y = x_ref[...] + gate_ref[...] * pt_ref[...].T
    ms = jnp.mean(y * y, axis=-1, keepdims=True)
    o_ref[...] = y * lax.rsqrt(ms + EPS) * g_ref[...]


def _final(x1, peer_t, modr, final_g):
    tr = 256
    return pl.pallas_call(
        _final_kernel,
        grid=(NTOK // tr,),
        in_specs=[pl.BlockSpec((tr, D_MODEL), lambda i: (i, 0)),
                  pl.BlockSpec((D_MODEL, tr), lambda i: (0, i)),
                  pl.BlockSpec((None, 1, D_MODEL), lambda i: ((i * tr) // SEQ * N_MOD + 5, 0, 0)),
                  pl.BlockSpec((1, D_MODEL), lambda i: (0, 0))],
        out_specs=pl.BlockSpec((tr, D_MODEL), lambda i: (i, 0)),
        out_shape=jax.ShapeDtypeStruct((NTOK, D_MODEL), F32),
        compiler_params=_params("arbitrary"),
        name="final_norm",
    )(x1, peer_t, modr, final_g)


def kernel(x, c, w_ada, b_ada, norm1_g, norm2_g, w_in, attn_sinks, rel_bias, gla_w_gk2, gla_b_gk2,
           gla_norm_g, w_out, peer_w_q, peer_keys, peer_u, peer_v, final_g):
    assert x.shape == (BATCH, SEQ, D_MODEL) and w_ada.shape[0] == 1
    x2 = x.reshape(NTOK, D_MODEL)
    c8 = jnp.pad(c, ((0, 8 - BATCH), (0, 0)))
    mod = _ada(c8, w_ada[0], b_ada[0][None, :])
    modr = mod[:BATCH].reshape(BATCH * N_MOD, 1, D_MODEL)

    wi = w_in[0]
    w_gout = wi[:, OFF_GOUT:OFF_GOUT + GLA_V]
    w_glow = jnp.pad(wi[:, OFF_GLOW:OFF_GOUT], ((0, 0), (0, 128 - GLA_RANK)))
    w2p = jnp.pad(gla_w_gk2[0], ((0, 128 - GLA_RANK), (0, 0)))

    h1 = _norm_mod(x2, norm1_g[0][None, :], modr, 1, 0)
    proj = _matmul(h1, wi, OFF_GLOW, BF, 1024, 512, "in_proj")
    gout = _matmul(h1, w_gout, GLA_V, BF, 1024, 512, "gout_proj")
    glow = _matmul(h1, w_glow, 128, F32, 1024, 128, "glow_proj")

    bias_tab = _bias_table(rel_bias, attn_sinks[0])
    attn = _attention(proj, bias_tab)
    gla = _gla(proj, gout, glow, w2p, gla_b_gk2[0][None, :], gla_norm_g[0][None, :])
    x1 = _outproj(attn, gla, w_out[0], x2, modr)

    h2 = _norm_mod(x1, norm2_g[0][None, :], modr, 4, 3)
    q = _matmul(h2, peer_w_q[0], PEER_HEADS * PEER_DKEY, F32, 1024, 512, "peer_q")
    k2, e2, n1, e1 = _route(q, peer_keys[0].astype(BF))
    peer_t = _peer(h2, peer_u[0].astype(BF), peer_v[0].T.astype(BF), k2, e2, n1, e1)
    out = _final(x1, peer_t, modr, final_g[None, :])
    return out.reshape(BATCH, SEQ, D_MODEL)
```

```python
import math

import jax
import jax.numpy as jnp
from jax import lax
from jax.experimental import pallas as pl
from jax.experimental.pallas import tpu as pltpu

BF = jnp.bfloat16
F32 = jnp.float32

D_MODEL = 4096
BATCH = 4
SEQ = 2048
NTOK = BATCH * SEQ
HEAD_DIM = 128
ATTN_HEADS = 16
ATTN_KV_HEADS = 2
ATTN_GROUP = 8
WINDOW = 128
BLOCK = 128
N_BUCKETS = 32
MAX_DISTANCE = 128
GLA_HEADS = 4
GLA_DV = 512
GLA_DK = 256
GLA_RANK = 16
GLA_TAU = 16.0
GLA_CHUNK = 64
ATTN_Q = 2048
ATTN_KV = 256
GLA_K = 1024
GLA_V = 2048
PEER_HEADS = 8
N_KEYS = 128
N_EXPERTS = N_KEYS * N_KEYS
PEER_TOPK = 16
PEER_DKEY = 256
N_MOD = 6
EPS = 1e-6
NEG_INF = -1e30

OFF_AQ, OFF_AK, OFF_AV, OFF_GQ, OFF_GK, OFF_GV, OFF_GLOW, OFF_GOUT = 0, 2048, 2304, 2560, 3584, 4608, 6656, 6672
HALF_K = GLA_K // 2

VMEM_LIMIT = 56 * 1024 * 1024

NT_DIMS = (((1,), (1,)), ((), ()))
TN_DIMS = (((0,), (0,)), ((), ()))


def _params(*sem, flags=None):
    return pltpu.CompilerParams(dimension_semantics=sem, vmem_limit_bytes=VMEM_LIMIT, flags=flags)


def _ada_kernel(c_ref, w_ref, b_ref, o_ref):
    c = c_ref[...]
    ca = c / (1.0 + jnp.exp(-c))
    o_ref[...] = jnp.dot(ca.astype(BF), w_ref[...].astype(BF), preferred_element_type=F32) + b_ref[...]


def _ada(c8, w_ada, b_ada):
    tn = 512
    n = N_MOD * D_MODEL
    return pl.pallas_call(
        _ada_kernel,
        grid=(n // tn,),
        in_specs=[pl.BlockSpec((8, D_MODEL), lambda j: (0, 0)),
                  pl.BlockSpec((D_MODEL, tn), lambda j: (0, j)),
                  pl.BlockSpec((1, tn), lambda j: (0, j))],
        out_specs=pl.BlockSpec((8, tn), lambda j: (0, j)),
        out_shape=jax.ShapeDtypeStruct((8, n), F32),
        compiler_params=_params("arbitrary"),
        name="ada",
    )(c8, w_ada, b_ada)


def _norm_mod_kernel(x_ref, g_ref, sc_ref, sh_ref, o_ref):
    x = x_ref[...]
    ms = jnp.mean(x * x, axis=-1, keepdims=True)
    y = x * lax.rsqrt(ms + EPS) * g_ref[...]
    o_ref[...] = (y * (1.0 + sc_ref[...]) + sh_ref[...]).astype(o_ref.dtype)


def _norm_mod(x2, g, modr, scale_idx, shift_idx):
    tr = 256
    return pl.pallas_call(
        _norm_mod_kernel,
        grid=(NTOK // tr,),
        in_specs=[pl.BlockSpec((tr, D_MODEL), lambda i: (i, 0)),
                  pl.BlockSpec((1, D_MODEL), lambda i: (0, 0)),
                  pl.BlockSpec((None, 1, D_MODEL), lambda i: ((i * tr) // SEQ * N_MOD + scale_idx, 0, 0)),
                  pl.BlockSpec((None, 1, D_MODEL), lambda i: ((i * tr) // SEQ * N_MOD + shift_idx, 0, 0))],
        out_specs=pl.BlockSpec((tr, D_MODEL), lambda i: (i, 0)),
        out_shape=jax.ShapeDtypeStruct((NTOK, D_MODEL), BF),
        compiler_params=_params("arbitrary"),
        name="norm_mod",
    )(x2, g, modr, modr)


def _mm_kernel(a_ref, w_ref, o_ref, wbf_ref):
    @pl.when(pl.program_id(1) == 0)
    def _():
        wbf_ref[...] = w_ref[...].astype(BF)

    o_ref[...] = jnp.dot(a_ref[...], wbf_ref[...], preferred_element_type=F32).astype(o_ref.dtype)


def _matmul(a, w, ncols, out_dtype, tm, tn, name):
    m, k = a.shape
    return pl.pallas_call(
        _mm_kernel,
        grid=(ncols // tn, m // tm),
        in_specs=[pl.BlockSpec((tm, k), lambda j, i: (i, 0)),
                  pl.BlockSpec((k, tn), lambda j, i: (0, j))],
        out_specs=pl.BlockSpec((tm, tn), lambda j, i: (i, j)),
        out_shape=jax.ShapeDtypeStruct((m, ncols), out_dtype),
        scratch_shapes=[pltpu.VMEM((k, tn), BF)],
        compiler_params=_params("arbitrary", "arbitrary"),
        name=name,
    )(a, w)


def _mm_t_kernel(a_ref, wt_ref, o_ref, wbf_ref):
    @pl.when(pl.program_id(1) == 0)
    def _():
        wbf_ref[...] = wt_ref[...].astype(BF)

    o_ref[...] = lax.dot_general(a_ref[...], wbf_ref[...], NT_DIMS, preferred_element_type=F32).astype(o_ref.dtype)


def _matmul_t(a, wt, row0, nrows, out_dtype, tm, tn, name):
    m, k = a.shape
    return pl.pallas_call(
        _mm_t_kernel,
        grid=(nrows // tn, m // tm),
        in_specs=[pl.BlockSpec((tm, k), lambda j, i: (i, 0)),
                  pl.BlockSpec((pl.Element(tn), pl.Element(k)), lambda j, i: (pl.multiple_of(row0 + j * tn, 8), 0))],
        out_specs=pl.BlockSpec((tm, tn), lambda j, i: (i, j)),
        out_shape=jax.ShapeDtypeStruct((m, nrows), out_dtype),
        scratch_shapes=[pltpu.VMEM((tn, k), BF)],
        compiler_params=_params("arbitrary", "arbitrary"),
        name=name,
    )(a, wt)


def _t5_bucket(dist):
    max_exact = N_BUCKETS // 2
    d = jnp.maximum(dist, 0)
    log_ratio = jnp.log(jnp.maximum(d, 1).astype(F32) / max_exact) / math.log(MAX_DISTANCE / max_exact)
    large = max_exact + (log_ratio * (N_BUCKETS - max_exact)).astype(jnp.int32)
    large = jnp.minimum(large, N_BUCKETS - 1)
    return jnp.where(d < max_exact, d, large)


def _bias_kernel(rel_ref, sink_ref, bucket_ref, o_ref):
    h = pl.program_id(0)
    bucket = bucket_ref[...]
    acc = jnp.full((BLOCK, 2 * BLOCK), NEG_INF, F32)
    for b in range(N_BUCKETS):
        acc = jnp.where(bucket == b, rel_ref[b, h], acc)
    col = lax.broadcasted_iota(jnp.int32, (BLOCK, 2 * BLOCK), 1)
    acc = jnp.where(col == 0, sink_ref[h], acc)
    o_ref[0, 0] = acc
    o_ref[1, 0] = jnp.where(jnp.logical_and(col < BLOCK, col > 0), NEG_INF, acc)


def _bias_table(rel_bias, sinks):
    qi = jnp.arange(BLOCK)[:, None]
    kj = jnp.arange(2 * BLOCK)[None, :]
    dist = qi + BLOCK - kj
    in_window = (dist >= 0) & (dist < WINDOW)
    bucket = jnp.where(in_window, _t5_bucket(dist), -1).astype(jnp.int32)
    tab = pl.pallas_call(
        _bias_kernel,
        grid=(ATTN_HEADS,),
        in_specs=[pl.BlockSpec(memory_space=pltpu.SMEM),
                  pl.BlockSpec(memory_space=pltpu.SMEM),
                  pl.BlockSpec((BLOCK, 2 * BLOCK), lambda h: (0, 0))],
        out_specs=pl.BlockSpec((2, 1, BLOCK, 2 * BLOCK), lambda h: (0, h, 0, 0)),
        out_shape=jax.ShapeDtypeStruct((2, ATTN_HEADS, BLOCK, 2 * BLOCK), F32),
        compiler_params=_params("arbitrary"),
        name="bias_table",
    )(rel_bias, sinks, bucket)
    return tab.reshape(2, ATTN_KV_HEADS, ATTN_GROUP * BLOCK, 2 * BLOCK)


def _attn_kernel(q_ref, kp_ref, kc_ref, vp_ref, vc_ref, bias_ref, o_ref):
    q = q_ref[...]
    qs = jnp.concatenate([q[:, g * HEAD_DIM:(g + 1) * HEAD_DIM] for g in range(ATTN_GROUP)], axis=0)
    row = lax.broadcasted_iota(jnp.int32, (BLOCK, HEAD_DIM), 0)
    kp = jnp.where(row == 0, 0.0, kp_ref[...].astype(F32)).astype(BF)
    vp = jnp.where(row == 0, 0.0, vp_ref[...].astype(F32)).astype(BF)
    kcat = jnp.concatenate([kp, kc_ref[...]], axis=0)
    vcat = jnp.concatenate([vp, vc_ref[...]], axis=0)
    s = lax.dot_general(qs, kcat, NT_DIMS, preferred_element_type=F32) * (HEAD_DIM ** -0.5) + bias_ref[...]
    m = jnp.max(s, axis=-1, keepdims=True)
    p = jnp.exp(s - m).astype(BF)
    vaug = jnp.concatenate([vcat, jnp.ones((2 * BLOCK, HEAD_DIM), BF)], axis=1)
    oa = jnp.dot(p, vaug, preferred_element_type=F32)
    o = oa[:, :HEAD_DIM] / oa[:, HEAD_DIM:]
    o_ref[...] = jnp.concatenate([o[g * BLOCK:(g + 1) * BLOCK, :] for g in range(ATTN_GROUP)], axis=1).astype(o_ref.dtype)


def _attention(proj, bias_tab):
    nb = SEQ // BLOCK
    gw = ATTN_GROUP * HEAD_DIM
    kblk = OFF_AK // HEAD_DIM
    vblk = OFF_AV // HEAD_DIM
    return pl.pallas_call(
        _attn_kernel,
        grid=(ATTN_KV_HEADS, BATCH, nb),
        in_specs=[pl.BlockSpec((BLOCK, gw), lambda h, b, n: (b * nb + n, h)),
                  pl.BlockSpec((BLOCK, HEAD_DIM), lambda h, b, n: (b * nb + jnp.maximum(n - 1, 0), kblk + h)),
                  pl.BlockSpec((BLOCK, HEAD_DIM), lambda h, b, n: (b * nb + n, kblk + h)),
                  pl.BlockSpec((BLOCK, HEAD_DIM), lambda h, b, n: (b * nb + jnp.maximum(n - 1, 0), vblk + h)),
                  pl.BlockSpec((BLOCK, HEAD_DIM), lambda h, b, n: (b * nb + n, vblk + h)),
                  pl.BlockSpec((None, None, gw, 2 * BLOCK), lambda h, b, n: (jnp.where(n == 0, 1, 0), h, 0, 0))],
        out_specs=pl.BlockSpec((BLOCK, gw), lambda h, b, n: (b * nb + n, h)),
        out_shape=jax.ShapeDtypeStruct((NTOK, ATTN_Q), BF),
        compiler_params=_params("arbitrary", "arbitrary", "arbitrary"),
        name="swa_attn",
    )(proj, proj, proj, proj, proj, bias_tab)


GLA_TB = 128


def _gla_kernel(q0_ref, q1_ref, k0_ref, k1_ref, v0_ref, v1_ref, v2_ref, v3_ref, go_ref, glow_ref, w2_ref, b2_ref,
                g_ref, o_ref, st_ref):
    q_refs, k_refs, v_refs = (q0_ref, q1_ref), (k0_ref, k1_ref), (v0_ref, v1_ref, v2_ref, v3_ref)

    @pl.when(pl.program_id(1) == 0)
    def _():
        st_ref[...] = jnp.zeros_like(st_ref)

    C = GLA_CHUNK
    ri = lax.broadcasted_iota(jnp.int32, (C, C), 0)
    ci = lax.broadcasted_iota(jnp.int32, (C, C), 1)
    tril = ri >= ci
    trilf = tril.astype(F32)
    for c in range(GLA_TB // C):
        rs = slice(c * C, (c + 1) * C)
        z = jnp.dot(glow_ref[rs, :], w2_ref[...], precision=lax.Precision.HIGHEST,
                    preferred_element_type=F32) + b2_ref[...]
        la = (jnp.minimum(z, 0.0) - jnp.log1p(jnp.exp(-jnp.abs(z)))) * (1.0 / GLA_TAU)
        bcum = jnp.dot(trilf, la, precision=lax.Precision.HIGHEST, preferred_element_type=F32)
        blast = bcum[C - 1:C, :]
        eb = jnp.exp(bcum)
        enb = jnp.exp(-bcum)
        edl = jnp.exp(blast - bcum)
        el = jnp.exp(blast)
        for h in range(GLA_HEADS):
            ks = slice(h * GLA_DK, (h + 1) * GLA_DK)
            vs = slice(h * GLA_DV, (h + 1) * GLA_DV)
            hs = slice((h % 2) * GLA_DK, (h % 2 + 1) * GLA_DK)
            q = q_refs[h // 2][rs, hs].astype(F32) * (GLA_DK ** -0.5)
            k = k_refs[h // 2][rs, hs].astype(F32)
            v = v_refs[h][rs, :]
            qd = (q * eb[:, ks]).astype(BF)
            ki = (k * enb[:, ks]).astype(BF)
            kd = (k * edl[:, ks]).astype(BF)
            a = lax.dot_general(qd, ki, NT_DIMS, preferred_element_type=F32)
            a = jnp.where(tril, a, 0.0).astype(BF)
            st = st_ref[h]
            o = (jnp.dot(a, v, preferred_element_type=F32)
                 + lax.dot_general(qd, st.astype(BF), NT_DIMS, preferred_element_type=F32))
            st_ref[h] = st * el[:, ks] + lax.dot_general(v, kd, TN_DIMS, preferred_element_type=F32)
            ms = jnp.mean(o * o, axis=-1, keepdims=True)
            y = o * lax.rsqrt(ms + EPS) * g_ref[...]
            gate = go_ref[rs, vs].astype(F32)
            y = y * (gate / (1.0 + jnp.exp(-gate)))
            o_ref[rs, vs] = y.astype(o_ref.dtype)


def _gla(proj, gout, glow, w2, b2, gnorm):
    nt = SEQ // GLA_TB

    def col(width, off):
        return pl.BlockSpec((GLA_TB, width), lambda b, t: (b * nt + t, off // width))

    return pl.pallas_call(
        _gla_kernel,
        grid=(BATCH, nt),
        in_specs=[col(HALF_K, OFF_GQ), col(HALF_K, OFF_GQ + HALF_K), col(HALF_K, OFF_GK), col(HALF_K, OFF_GK + HALF_K),
                  col(GLA_DV, OFF_GV), col(GLA_DV, OFF_GV + GLA_DV), col(GLA_DV, OFF_GV + 2 * GLA_DV),
                  col(GLA_DV, OFF_GV + 3 * GLA_DV),
                  pl.BlockSpec((GLA_TB, GLA_V), lambda b, t: (b * nt + t, 0)),
                  pl.BlockSpec((GLA_TB, GLA_RANK), lambda b, t: (b * nt + t, 0)),
                  pl.BlockSpec((GLA_RANK, GLA_K), lambda b, t: (0, 0)),
                  pl.BlockSpec((1, GLA_K), lambda b, t: (0, 0)),
                  pl.BlockSpec((1, GLA_DV), lambda b, t: (0, 0))],
        out_specs=pl.BlockSpec((GLA_TB, GLA_V), lambda b, t: (b * nt + t, 0)),
        out_shape=jax.ShapeDtypeStruct((NTOK, GLA_V), BF),
        scratch_shapes=[pltpu.VMEM((GLA_HEADS, GLA_DV, GLA_DK), F32)],
        compiler_params=_params("arbitrary", "arbitrary"),
        name="gla",
    )(proj, proj, proj, proj, proj, proj, proj, proj, gout, glow, w2, b2, gnorm)


def _outproj_kernel(a_ref, g_ref, w_ref, x_ref, gate_ref, o_ref, wbf_ref):
    @pl.when(pl.program_id(1) == 0)
    def _():
        wbf_ref[...] = w_ref[...].astype(BF)

    acc = (jnp.dot(a_ref[...], wbf_ref[0:ATTN_Q, :], preferred_element_type=F32)
           + jnp.dot(g_ref[...], wbf_ref[ATTN_Q:ATTN_Q + GLA_V, :], preferred_element_type=F32))
    o_ref[...] = x_ref[...] + gate_ref[...] * acc


def _outproj(attn, gla, w_out, x2, modr):
    tm, tn = 1024, 512
    return pl.pallas_call(
        _outproj_kernel,
        grid=(D_MODEL // tn, NTOK // tm),
        in_specs=[pl.BlockSpec((tm, ATTN_Q), lambda j, i: (i, 0)),
                  pl.BlockSpec((tm, GLA_V), lambda j, i: (i, 0)),
                  pl.BlockSpec((ATTN_Q + GLA_V, tn), lambda j, i: (0, j)),
                  pl.BlockSpec((tm, tn), lambda j, i: (i, j)),
                  pl.BlockSpec((None, 1, tn), lambda j, i: ((i * tm) // SEQ * N_MOD + 2, 0, j))],
        out_specs=pl.BlockSpec((tm, tn), lambda j, i: (i, j)),
        out_shape=jax.ShapeDtypeStruct((NTOK, D_MODEL), F32),
        scratch_shapes=[pltpu.VMEM((ATTN_Q + GLA_V, tn), BF)],
        compiler_params=_params("arbitrary", "arbitrary"),
        name="outproj",
    )(attn, gla, w_out, x2, modr)


ROUTE_TR = 512
LANE = 128
NOT_TOP = 99.0
CAND_ROWS = 8 * (PEER_TOPK + 1)


def _extract_top(work, keys, exact):
    for _ in range(PEER_TOPK):
        m = jnp.max(work, axis=0, keepdims=True)
        sel = work == m
        if exact:
            first = jnp.min(jnp.where(sel, keys, 1e9), axis=0, keepdims=True)
            sel = keys == first
        work = jnp.where(sel, -jnp.inf, work)
        yield m, sel, work


def _route_compute(s1, s2, exact):
    t = s1.shape[1]
    s = jnp.concatenate([s1, s2], axis=1)
    rows = lax.broadcasted_iota(jnp.int32, s.shape, 0).astype(F32)
    rank = jnp.full(s.shape, NOT_TOP, F32)
    vals = []
    work = s
    for r, (m, sel, work) in enumerate(_extract_top(s, rows, exact)):
        rank = jnp.where(sel, float(r), rank)
        vals.append(m)
    removed = jnp.sum(jnp.where(work == -jnp.inf, 1.0, 0.0), axis=0, keepdims=True)
    bad = jnp.max(jnp.abs(removed - PEER_TOPK))
    rank1, rank2 = rank[:, :t], rank[:, t:]
    v1 = [v[:, :t] for v in vals]
    v2 = [v[:, t:] for v in vals]
    t2a = jnp.concatenate(v2[:8], axis=0)
    t2b = jnp.concatenate(v2[8:], axis=0)
    cand = jnp.concatenate([v1[0] + t2a, v1[0] + t2b] + [v1[r] + t2a for r in range(1, PEER_TOPK)], axis=0)
    i = lax.broadcasted_iota(jnp.int32, (CAND_ROWS, t), 0)
    flat = jnp.where(i < 16, i, 2 * i - (i & 7) - 16).astype(F32)
    work = cand
    for _, _, work in _extract_top(cand, flat, exact):
        pass
    chosen = work == -jnp.inf
    cnt = jnp.where(chosen, 1.0, 0.0)
    bad = jnp.maximum(bad, jnp.max(jnp.abs(jnp.sum(cnt, axis=0, keepdims=True) - PEER_TOPK)))
    top = v1[0] + v2[0]
    z = jnp.sum(jnp.where(chosen, jnp.exp(cand - top), 0.0), axis=0, keepdims=True)
    n = [jnp.sum(cnt[0:16], axis=0, keepdims=True)]
    n += [jnp.sum(cnt[8 * (r + 1):8 * (r + 2)], axis=0, keepdims=True) for r in range(1, PEER_TOPK)]
    n1 = jnp.zeros_like(s1)
    for r in range(PEER_TOPK):
        n1 = jnp.where(rank1 == float(r), n[r], n1)
    e2 = jnp.exp(s2 - v2[0])
    e1 = jnp.exp(s1 - v1[0]) / z
    return rank2, e2, n1, e1, bad


def _route_kernel(q_ref, keys_ref, k2_ref, e2_ref, n1_ref, e1_ref):
    q = q_ref[...].astype(BF)
    s1 = lax.dot_general(keys_ref[0], q[:, :N_KEYS], NT_DIMS, preferred_element_type=F32)
    s2 = lax.dot_general(keys_ref[1], q[:, N_KEYS:], NT_DIMS, preferred_element_type=F32)

    def emit(rank2, e2, n1, e1):
        k2_ref[...] = rank2.astype(k2_ref.dtype)
        e2_ref[...] = e2.astype(e2_ref.dtype)
        for c in range(s1.shape[1] // LANE):
            n1_ref[c] = n1[:, c * LANE:(c + 1) * LANE]
            e1_ref[c] = e1[:, c * LANE:(c + 1) * LANE]

    *fast, bad = _route_compute(s1, s2, exact=False)
    emit(*fast)

    @pl.when(bad > 0.0)
    def _():
        *slow, _ = _route_compute(s1, s2, exact=True)
        emit(*slow)


def _route(q, keys):
    tr = ROUTE_TR
    shp_bf = jax.ShapeDtypeStruct((PEER_HEADS, N_KEYS, NTOK), BF)
    ospec = pl.BlockSpec((None, N_KEYS, tr), lambda i, h: (h, 0, i))
    shp_a = jax.ShapeDtypeStruct((PEER_HEADS, NTOK // LANE, N_KEYS, LANE), F32)
    aspec = pl.BlockSpec((None, tr // LANE, N_KEYS, LANE), lambda i, h: (h, i, 0, 0))
    return pl.pallas_call(
        _route_kernel,
        grid=(NTOK // tr, PEER_HEADS),
        in_specs=[pl.BlockSpec((tr, PEER_DKEY), lambda i, h: (i, h)),
                  pl.BlockSpec((None, 2, N_KEYS, PEER_DKEY // 2), lambda i, h: (h, 0, 0, 0))],
        out_specs=[ospec, ospec, aspec, aspec],
        out_shape=[shp_bf, shp_bf, shp_a, shp_a],
        compiler_params=_params("arbitrary", "arbitrary"),
        name="peer_route",
    )(q, keys)


PEER_TM = 512
PEER_TE = 512
PEER_NJ = N_EXPERTS // PEER_TE
PEER_RC = 256
PEER_NC = D_MODEL // PEER_RC
PEER_PERIOD = PEER_NJ + 1 + PEER_NC
PEER_VMEM_LIMIT = 60 * 1024 * 1024


def _peer_step(h_ref, u_ref, act_w, k2_ref, e2_ref, n1_ref, e1_ref, act_r, wt_ref, jb):
    half_e, half_t = PEER_TE // 2, PEER_TM // 2
    for k in range(PEER_TE // N_KEYS):
        if act_w is not None:
            e0, t0 = half_e * (k // 2), half_t * (k % 2)
            act_w[e0:e0 + half_e, t0:t0 + half_t] = lax.dot_general(
                u_ref[e0:e0 + half_e, :], h_ref[t0:t0 + half_t, :], NT_DIMS, preferred_element_type=F32)
        if act_r is None:
            continue
        a = jb * (PEER_TE // N_KEYS) + k
        es = slice(k * N_KEYS, (k + 1) * N_KEYS)
        for lc in range(PEER_TM // LANE):
            ls = slice(lc * LANE, (lc + 1) * LANE)
            g = None
            for h in range(PEER_HEADS):
                n1 = jnp.broadcast_to(n1_ref[h, lc, pl.ds(a, 1), :], (N_KEYS, LANE)).astype(BF)
                e1 = jnp.broadcast_to(e1_ref[h, lc, pl.ds(a, 1), :], (N_KEYS, LANE)).astype(BF)
                t = jnp.where(k2_ref[h, :, ls] < n1, e2_ref[h, :, ls], jnp.zeros((), BF)) * e1
                g = t if g is None else g + t
            x = act_r[es, ls]
            erf = lax.erf((x * (2.0 ** -0.5)).astype(BF))
            wt_ref[jb, es, ls] = g * ((0.5 * x).astype(BF) * (1.0 + erf))


def _peer_kernel(h_ref, u_ref, v_ref, k2_ref, e2_ref, n1_ref, e1_ref, o_ref, act0, act1, wt_ref):
    r = pl.program_id(0) % PEER_PERIOD
    jb = jnp.maximum(r - 1, 0)
    gates = (k2_ref, e2_ref, n1_ref, e1_ref)

    @pl.when(r == 0)
    def _():
        _peer_step(h_ref, u_ref, act0, *gates, None, wt_ref, jb)

    @pl.when(jnp.logical_and(jnp.logical_and(r > 0, r < PEER_NJ), r % 2 == 1))
    def _():
        _peer_step(h_ref, u_ref, act1, *gates, act0, wt_ref, jb)

    @pl.when(jnp.logical_and(jnp.logical_and(r > 0, r < PEER_NJ), r % 2 == 0))
    def _():
        _peer_step(h_ref, u_ref, act0, *gates, act1, wt_ref, jb)

    @pl.when(r == PEER_NJ)
    def _():
        _peer_step(h_ref, u_ref, None, *gates, act1 if PEER_NJ % 2 == 0 else act0, wt_ref, jb)

    @pl.when(r > PEER_NJ)
    def _():
        wt = wt_ref[...].reshape(N_EXPERTS, PEER_TM)
        o_ref[...] = jnp.dot(v_ref[...], wt, preferred_element_type=F32)


def _peer(h2, u, vt, k2, e2, n1, e1):
    tm, te, nj, rc, period = PEER_TM, PEER_TE, PEER_NJ, PEER_RC, PEER_PERIOD

    def tile(s):
        return s // period

    def expert_step(s):
        return jnp.minimum(s % period, nj - 1)

    def row_step(s):
        return jnp.maximum(s % period - (nj + 1), 0)

    rspec = pl.BlockSpec((PEER_HEADS, N_KEYS, tm), lambda s: (0, 0, tile(s)), pipeline_mode=pl.Buffered(1))
    aspec = pl.BlockSpec((PEER_HEADS, tm // LANE, N_KEYS, LANE), lambda s: (0, tile(s), 0, 0),
                         pipeline_mode=pl.Buffered(1))
    return pl.pallas_call(
        _peer_kernel,
        grid=((NTOK // tm) * period,),
        in_specs=[pl.BlockSpec((tm, D_MODEL), lambda s: (tile(s), 0), pipeline_mode=pl.Buffered(1)),
                  pl.BlockSpec((te, D_MODEL), lambda s: (expert_step(s), 0)),
                  pl.BlockSpec((rc, N_EXPERTS), lambda s: (row_step(s), 0)),
                  rspec, rspec, aspec, aspec],
        out_specs=pl.BlockSpec((rc, tm), lambda s: (row_step(s), tile(s))),
        out_shape=jax.ShapeDtypeStruct((D_MODEL, NTOK), F32),
        scratch_shapes=[pltpu.VMEM((te, tm), F32), pltpu.VMEM((te, tm), F32), pltpu.VMEM((nj, te, tm), BF)],
        compiler_params=pltpu.CompilerParams(dimension_semantics=("arbitrary",), vmem_limit_bytes=PEER_VMEM_LIMIT),
        name="peer_dense",
    )(h2, u, vt, k2, e2, n1, e1)


def _final_kernel(x_ref, pt_ref, gate_ref, g_ref, o_ref):
    y = x_ref[...] + gate_ref[...] * pt_ref[...].T
    ms = jnp.mean(y * y, axis=-1, keepdims=True)
    o_ref[...] = y * lax.rsqrt(ms + EPS) * g_ref[...]


def _final(x1, peer_t, modr, final_g):
    tr = 256
    return pl.pallas_call(
        _final_kernel,
        grid=(NTOK // tr,),
        in_specs=[pl.BlockSpec((tr, D_MODEL), lambda i: (i, 0)),
                  pl.BlockSpec((D_MODEL, tr), lambda i: (0, i)),
                  pl.BlockSpec((None, 1, D_MODEL), lambda i: ((i * tr) // SEQ * N_MOD + 5, 0, 0)),
                  pl.BlockSpec((1, D_MODEL), lambda i: (0, 0))],
        out_specs=pl.BlockSpec((tr, D_MODEL), lambda i: (i, 0)),
        out_shape=jax.ShapeDtypeStruct((NTOK, D_MODEL), F32),
        compiler_params=_params("arbitrary"),
        name="final_norm",
    )(x1, peer_t, modr, final_g)


def kernel(x, c, w_ada, b_ada, norm1_g, norm2_g, w_in, attn_sinks, rel_bias, gla_w_gk2, gla_b_gk2,
           gla_norm_g, w_out, peer_w_q, peer_keys, peer_u, peer_v, final_g):
    assert x.shape == (BATCH, SEQ, D_MODEL) and w_ada.shape[0] == 1
    x2 = x.reshape(NTOK, D_MODEL)
    c8 = jnp.pad(c, ((0, 8 - BATCH), (0, 0)))
    mod = _ada(c8, w_ada[0], b_ada[0][None, :])
    modr = mod[:BATCH].reshape(BATCH * N_MOD, 1, D_MODEL)

    wt = jnp.swapaxes(w_in, 1, 2)[0]

    h1 = _norm_mod(x2, norm1_g[0][None, :], modr, 1, 0)
    proj = _matmul_t(h1, wt, 0, OFF_GLOW, BF, 1024, 512, "in_proj")
    gout = _matmul_t(h1, wt, OFF_GOUT, GLA_V, BF, 1024, 512, "gout_proj")
    glow = _matmul_t(h1, wt, OFF_GLOW, GLA_RANK, F32, 1024, GLA_RANK, "glow_proj")

    bias_tab = _bias_table(rel_bias, attn_sinks[0])
    attn = _attention(proj, bias_tab)
    gla = _gla(proj, gout, glow, gla_w_gk2[0], gla_b_gk2[0][None, :], gla_norm_g[0][None, :])
    x1 = _outproj(attn, gla, w_out[0], x2, modr)

    h2 = _norm_mod(x1, norm2_g[0][None, :], modr, 4, 3)
    q = _matmul(h2, peer_w_q[0], PEER_HEADS * PEER_DKEY, F32, 1024, 512, "peer_q")
    k2, e2, n1, e1 = _route(q, peer_keys[0].astype(BF))
    peer_t = _peer(h2, peer_u[0].astype(BF), peer_v[0].T.astype(BF), k2, e2, n1, e1)
    out = _final(x1, peer_t, modr, final_g[None, :])
    return out.reshape(BATCH, SEQ, D_MODEL)
```

```python
import math

import jax
import jax.numpy as jnp
from jax import lax
from jax.experimental import pallas as pl
from jax.experimental.pallas import tpu as pltpu

BF = jnp.bfloat16
F32 = jnp.float32

D_MODEL = 4096
BATCH = 4
SEQ = 2048
NTOK = BATCH * SEQ
HEAD_DIM = 128
ATTN_HEADS = 16
ATTN_KV_HEADS = 2
ATTN_GROUP = 8
WINDOW = 128
BLOCK = 128
N_BUCKETS = 32
MAX_DISTANCE = 128
GLA_HEADS = 4
GLA_DV = 512
GLA_DK = 256
GLA_RANK = 16
GLA_TAU = 16.0
GLA_CHUNK = 64
ATTN_Q = 2048
ATTN_KV = 256
GLA_K = 1024
GLA_V = 2048
PEER_HEADS = 8
N_KEYS = 128
N_EXPERTS = N_KEYS * N_KEYS
PEER_TOPK = 16
PEER_DKEY = 256
N_MOD = 6
EPS = 1e-6
NEG_INF = -1e30

OFF_AQ, OFF_AK, OFF_AV, OFF_GQ, OFF_GK, OFF_GV, OFF_GLOW, OFF_GOUT = 0, 2048, 2304, 2560, 3584, 4608, 6656, 6672
HALF_K = GLA_K // 2

VMEM_LIMIT = 56 * 1024 * 1024

NT_DIMS = (((1,), (1,)), ((), ()))
TN_DIMS = (((0,), (0,)), ((), ()))


def _params(*sem, flags=None):
    return pltpu.CompilerParams(dimension_semantics=sem, vmem_limit_bytes=VMEM_LIMIT, flags=flags)


def _ada_kernel(c_ref, w_ref, b_ref, o_ref):
    c = c_ref[...]
    ca = c / (1.0 + jnp.exp(-c))
    o_ref[...] = jnp.dot(ca.astype(BF), w_ref[...].astype(BF), preferred_element_type=F32) + b_ref[...]


def _ada(c8, w_ada, b_ada):
    tn = 512
    n = N_MOD * D_MODEL
    return pl.pallas_call(
        _ada_kernel,
        grid=(n // tn,),
        in_specs=[pl.BlockSpec((8, D_MODEL), lambda j: (0, 0)),
                  pl.BlockSpec((D_MODEL, tn), lambda j: (0, j)),
                  pl.BlockSpec((1, tn), lambda j: (0, j))],
        out_specs=pl.BlockSpec((8, tn), lambda j: (0, j)),
        out_shape=jax.ShapeDtypeStruct((8, n), F32),
        compiler_params=_params("arbitrary"),
        name="ada",
    )(c8, w_ada, b_ada)


def _norm_mod_kernel(x_ref, g_ref, sc_ref, sh_ref, o_ref):
    x = x_ref[...]
    ms = jnp.mean(x * x, axis=-1, keepdims=True)
    y = x * lax.rsqrt(ms + EPS) * g_ref[...]
    o_ref[...] = (y * (1.0 + sc_ref[...]) + sh_ref[...]).astype(o_ref.dtype)


def _norm_mod(x2, g, modr, scale_idx, shift_idx):
    tr = 256
    return pl.pallas_call(
        _norm_mod_kernel,
        grid=(NTOK // tr,),
        in_specs=[pl.BlockSpec((tr, D_MODEL), lambda i: (i, 0)),
                  pl.BlockSpec((1, D_MODEL), lambda i: (0, 0)),
                  pl.BlockSpec((None, 1, D_MODEL), lambda i: ((i * tr) // SEQ * N_MOD + scale_idx, 0, 0)),
                  pl.BlockSpec((None, 1, D_MODEL), lambda i: ((i * tr) // SEQ * N_MOD + shift_idx, 0, 0))],
        out_specs=pl.BlockSpec((tr, D_MODEL), lambda i: (i, 0)),
        out_shape=jax.ShapeDtypeStruct((NTOK, D_MODEL), BF),
        compiler_params=_params("arbitrary"),
        name="norm_mod",
    )(x2, g, modr, modr)


def _mm_kernel(a_ref, w_ref, o_ref, wbf_ref):
    @pl.when(pl.program_id(1) == 0)
    def _():
        wbf_ref[...] = w_ref[...].astype(BF)

    o_ref[...] = jnp.dot(a_ref[...], wbf_ref[...], preferred_element_type=F32).astype(o_ref.dtype)


def _matmul(a, w, ncols, out_dtype, tm, tn, name):
    m, k = a.shape
    return pl.pallas_call(
        _mm_kernel,
        grid=(ncols // tn, m // tm),
        in_specs=[pl.BlockSpec((tm, k), lambda j, i: (i, 0)),
                  pl.BlockSpec((k, tn), lambda j, i: (0, j))],
        out_specs=pl.BlockSpec((tm, tn), lambda j, i: (i, j)),
        out_shape=jax.ShapeDtypeStruct((m, ncols), out_dtype),
        scratch_shapes=[pltpu.VMEM((k, tn), BF)],
        compiler_params=_params("arbitrary", "arbitrary"),
        name=name,
    )(a, w)


def _mm_t_kernel(a_ref, wt_ref, o_ref, wbf_ref):
    @pl.when(pl.program_id(1) == 0)
    def _():
        wbf_ref[...] = wt_ref[...].astype(BF)

    o_ref[...] = lax.dot_general(a_ref[...], wbf_ref[...], NT_DIMS, preferred_element_type=F32).astype(o_ref.dtype)


def _matmul_t(a, wt, row0, nrows, out_dtype, tm, tn, name):
    m, k = a.shape
    return pl.pallas_call(
        _mm_t_kernel,
        grid=(nrows // tn, m // tm),
        in_specs=[pl.BlockSpec((tm, k), lambda j, i: (i, 0)),
                  pl.BlockSpec((pl.Element(tn), pl.Element(k)), lambda j, i: (pl.multiple_of(row0 + j * tn, 8), 0))],
        out_specs=pl.BlockSpec((tm, tn), lambda j, i: (i, j)),
        out_shape=jax.ShapeDtypeStruct((m, nrows), out_dtype),
        scratch_shapes=[pltpu.VMEM((tn, k), BF)],
        compiler_params=_params("arbitrary", "arbitrary"),
        name=name,
    )(a, wt)


def _t5_bucket(dist):
    max_exact = N_BUCKETS // 2
    d = jnp.maximum(dist, 0)
    log_ratio = jnp.log(jnp.maximum(d, 1).astype(F32) / max_exact) / math.log(MAX_DISTANCE / max_exact)
    large = max_exact + (log_ratio * (N_BUCKETS - max_exact)).astype(jnp.int32)
    large = jnp.minimum(large, N_BUCKETS - 1)
    return jnp.where(d < max_exact, d, large)


def _bias_kernel(rel_ref, sink_ref, bucket_ref, o_ref):
    h = pl.program_id(0)
    bucket = bucket_ref[...]
    acc = jnp.full((BLOCK, 2 * BLOCK), NEG_INF, F32)
    for b in range(N_BUCKETS):
        acc = jnp.where(bucket == b, rel_ref[b, h], acc)
    col = lax.broadcasted_iota(jnp.int32, (BLOCK, 2 * BLOCK), 1)
    acc = jnp.where(col == 0, sink_ref[h], acc)
    o_ref[0, 0] = acc
    o_ref[1, 0] = jnp.where(jnp.logical_and(col < BLOCK, col > 0), NEG_INF, acc)


def _bias_table(rel_bias, sinks):
    qi = jnp.arange(BLOCK)[:, None]
    kj = jnp.arange(2 * BLOCK)[None, :]
    dist = qi + BLOCK - kj
    in_window = (dist >= 0) & (dist < WINDOW)
    bucket = jnp.where(in_window, _t5_bucket(dist), -1).astype(jnp.int32)
    tab = pl.pallas_call(
        _bias_kernel,
        grid=(ATTN_HEADS,),
        in_specs=[pl.BlockSpec(memory_space=pltpu.SMEM),
                  pl.BlockSpec(memory_space=pltpu.SMEM),
                  pl.BlockSpec((BLOCK, 2 * BLOCK), lambda h: (0, 0))],
        out_specs=pl.BlockSpec((2, 1, BLOCK, 2 * BLOCK), lambda h: (0, h, 0, 0)),
        out_shape=jax.ShapeDtypeStruct((2, ATTN_HEADS, BLOCK, 2 * BLOCK), F32),
        compiler_params=_params("arbitrary"),
        name="bias_table",
    )(rel_bias, sinks, bucket)
    return tab.reshape(2, ATTN_KV_HEADS, ATTN_GROUP * BLOCK, 2 * BLOCK)


def _attn_kernel(q_ref, kp_ref, kc_ref, vp_ref, vc_ref, bias_ref, o_ref):
    row = lax.broadcasted_iota(jnp.int32, (BLOCK, HEAD_DIM), 0)
    gw = ATTN_GROUP * HEAD_DIM
    for h in range(ATTN_KV_HEADS):
        hs = slice(h * HEAD_DIM, (h + 1) * HEAD_DIM)
        q = q_ref[:, h * gw:(h + 1) * gw]
        qs = jnp.concatenate([q[:, g * HEAD_DIM:(g + 1) * HEAD_DIM] for g in range(ATTN_GROUP)], axis=0)
        kp = jnp.where(row == 0, 0.0, kp_ref[:, hs].astype(F32)).astype(BF)
        vp = jnp.where(row == 0, 0.0, vp_ref[:, hs].astype(F32)).astype(BF)
        kcat = jnp.concatenate([kp, kc_ref[:, hs]], axis=0)
        vcat = jnp.concatenate([vp, vc_ref[:, hs]], axis=0)
        s = lax.dot_general(qs, kcat, NT_DIMS, preferred_element_type=F32) * (HEAD_DIM ** -0.5) + bias_ref[h]
        m = jnp.max(s, axis=-1, keepdims=True)
        p = jnp.exp(s - m).astype(BF)
        vaug = jnp.concatenate([vcat, jnp.ones((2 * BLOCK, HEAD_DIM), BF)], axis=1)
        oa = jnp.dot(p, vaug, preferred_element_type=F32)
        o = oa[:, :HEAD_DIM] / oa[:, HEAD_DIM:]
        o_ref[:, h * gw:(h + 1) * gw] = jnp.concatenate(
            [o[g * BLOCK:(g + 1) * BLOCK, :] for g in range(ATTN_GROUP)], axis=1).astype(o_ref.dtype)


def _attention(proj, bias_tab):
    nb = SEQ // BLOCK
    gw = ATTN_GROUP * HEAD_DIM
    kvw = ATTN_KV_HEADS * HEAD_DIM
    kblk = OFF_AK // kvw
    vblk = OFF_AV // kvw
    return pl.pallas_call(
        _attn_kernel,
        grid=(BATCH, nb),
        in_specs=[pl.BlockSpec((BLOCK, ATTN_Q), lambda b, n: (b * nb + n, 0)),
                  pl.BlockSpec((BLOCK, kvw), lambda b, n: (b * nb + jnp.maximum(n - 1, 0), kblk)),
                  pl.BlockSpec((BLOCK, kvw), lambda b, n: (b * nb + n, kblk)),
                  pl.BlockSpec((BLOCK, kvw), lambda b, n: (b * nb + jnp.maximum(n - 1, 0), vblk)),
                  pl.BlockSpec((BLOCK, kvw), lambda b, n: (b * nb + n, vblk)),
                  pl.BlockSpec((None, ATTN_KV_HEADS, gw, 2 * BLOCK), lambda b, n: (jnp.where(n == 0, 1, 0), 0, 0, 0))],
        out_specs=pl.BlockSpec((BLOCK, ATTN_Q), lambda b, n: (b * nb + n, 0)),
        out_shape=jax.ShapeDtypeStruct((NTOK, ATTN_Q), BF),
        compiler_params=_params("arbitrary", "arbitrary"),
        name="swa_attn",
    )(proj, proj, proj, proj, proj, bias_tab)


GLA_TB = 128


def _gla_kernel(q0_ref, q1_ref, k0_ref, k1_ref, v0_ref, v1_ref, v2_ref, v3_ref, go_ref, glow_ref, w2_ref, b2_ref,
                g_ref, o_ref, st_ref):
    q_refs, k_refs, v_refs = (q0_ref, q1_ref), (k0_ref, k1_ref), (v0_ref, v1_ref, v2_ref, v3_ref)

    @pl.when(pl.program_id(1) == 0)
    def _():
        st_ref[...] = jnp.zeros_like(st_ref)

    C = GLA_CHUNK
    ri = lax.broadcasted_iota(jnp.int32, (C, C), 0)
    ci = lax.broadcasted_iota(jnp.int32, (C, C), 1)
    tril = ri >= ci
    trilf = tril.astype(F32)
    for c in range(GLA_TB // C):
        rs = slice(c * C, (c + 1) * C)
        z = jnp.dot(glow_ref[rs, :], w2_ref[...], precision=lax.Precision.HIGHEST,
                    preferred_element_type=F32) + b2_ref[...]
        la = (jnp.minimum(z, 0.0) - jnp.log1p(jnp.exp(-jnp.abs(z)))) * (1.0 / GLA_TAU)
        bcum = jnp.dot(trilf, la, precision=lax.Precision.HIGHEST, preferred_element_type=F32)
        blast = bcum[C - 1:C, :]
        eb = jnp.exp(bcum)
        enb = jnp.exp(-bcum)
        edl = jnp.exp(blast - bcum)
        el = jnp.exp(blast)
        for h in range(GLA_HEADS):
            ks = slice(h * GLA_DK, (h + 1) * GLA_DK)
            vs = slice(h * GLA_DV, (h + 1) * GLA_DV)
            hs = slice((h % 2) * GLA_DK, (h % 2 + 1) * GLA_DK)
            q = q_refs[h // 2][rs, hs].astype(F32) * (GLA_DK ** -0.5)
            k = k_refs[h // 2][rs, hs].astype(F32)
            v = v_refs[h][rs, :]
            qd = (q * eb[:, ks]).astype(BF)
            ki = (k * enb[:, ks]).astype(BF)
            kd = (k * edl[:, ks]).astype(BF)
            a = lax.dot_general(qd, ki, NT_DIMS, preferred_element_type=F32)
            a = jnp.where(tril, a, 0.0).astype(BF)
            st = st_ref[h]
            o = (jnp.dot(a, v, preferred_element_type=F32)
                 + lax.dot_general(qd, st.astype(BF), NT_DIMS, preferred_element_type=F32))
            st_ref[h] = st * el[:, ks] + lax.dot_general(v, kd, TN_DIMS, preferred_element_type=F32)
            ms = jnp.mean(o * o, axis=-1, keepdims=True)
            y = o * lax.rsqrt(ms + EPS) * g_ref[...]
            gate = go_ref[rs, vs].astype(F32)
            y = y * (gate / (1.0 + jnp.exp(-gate)))
            o_ref[rs, vs] = y.astype(o_ref.dtype)


def _gla(proj, gout, glow, w2, b2, gnorm):
    nt = SEQ // GLA_TB

    def col(width, off):
        return pl.BlockSpec((GLA_TB, width), lambda b, t: (b * nt + t, off // width))

    return pl.pallas_call(
        _gla_kernel,
        grid=(BATCH, nt),
        in_specs=[col(HALF_K, OFF_GQ), col(HALF_K, OFF_GQ + HALF_K), col(HALF_K, OFF_GK), col(HALF_K, OFF_GK + HALF_K),
                  col(GLA_DV, OFF_GV), col(GLA_DV, OFF_GV + GLA_DV), col(GLA_DV, OFF_GV + 2 * GLA_DV),
                  col(GLA_DV, OFF_GV + 3 * GLA_DV),
                  pl.BlockSpec((GLA_TB, GLA_V), lambda b, t: (b * nt + t, 0)),
                  pl.BlockSpec((GLA_TB, GLA_RANK), lambda b, t: (b * nt + t, 0)),
                  pl.BlockSpec((GLA_RANK, GLA_K), lambda b, t: (0, 0)),
                  pl.BlockSpec((1, GLA_K), lambda b, t: (0, 0)),
                  pl.BlockSpec((1, GLA_DV), lambda b, t: (0, 0))],
        out_specs=pl.BlockSpec((GLA_TB, GLA_V), lambda b, t: (b * nt + t, 0)),
        out_shape=jax.ShapeDtypeStruct((NTOK, GLA_V), BF),
        scratch_shapes=[pltpu.VMEM((GLA_HEADS, GLA_DV, GLA_DK), F32)],
        compiler_params=_params("arbitrary", "arbitrary"),
        name="gla",
    )(proj, proj, proj, proj, proj, proj, proj, proj, gout, glow, w2, b2, gnorm)


def _outproj_kernel(a_ref, g_ref, w_ref, x_ref, gate_ref, o_ref, wbf_ref):
    @pl.when(pl.program_id(1) == 0)
    def _():
        wbf_ref[...] = w_ref[...].astype(BF)

    acc = (jnp.dot(a_ref[...], wbf_ref[0:ATTN_Q, :], preferred_element_type=F32)
           + jnp.dot(g_ref[...], wbf_ref[ATTN_Q:ATTN_Q + GLA_V, :], preferred_element_type=F32))
    o_ref[...] = x_ref[...] + gate_ref[...] * acc


def _outproj(attn, gla, w_out, x2, modr):
    tm, tn = 1024, 512
    return pl.pallas_call(
        _outproj_kernel,
        grid=(D_MODEL // tn, NTOK // tm),
        in_specs=[pl.BlockSpec((tm, ATTN_Q), lambda j, i: (i, 0)),
                  pl.BlockSpec((tm, GLA_V), lambda j, i: (i, 0)),
                  pl.BlockSpec((ATTN_Q + GLA_V, tn), lambda j, i: (0, j)),
                  pl.BlockSpec((tm, tn), lambda j, i: (i, j)),
                  pl.BlockSpec((None, 1, tn), lambda j, i: ((i * tm) // SEQ * N_MOD + 2, 0, j))],
        out_specs=pl.BlockSpec((tm, tn), lambda j, i: (i, j)),
        out_shape=jax.ShapeDtypeStruct((NTOK, D_MODEL), F32),
        scratch_shapes=[pltpu.VMEM((ATTN_Q + GLA_V, tn), BF)],
        compiler_params=_params("arbitrary", "arbitrary"),
        name="outproj",
    )(attn, gla, w_out, x2, modr)


ROUTE_TR = 512
LANE = 128
NOT_TOP = 99.0
CAND_ROWS = 8 * (PEER_TOPK + 1)


def _top_values(s):
    vals = [jnp.max(s, axis=0, keepdims=True)]
    for _ in range(PEER_TOPK - 1):
        vals.append(jnp.max(jnp.where(s < vals[-1], s, -jnp.inf), axis=0, keepdims=True))
    return vals


def _take_top_in_order(work, keys):
    step = jnp.full(work.shape, NOT_TOP, F32)
    vals = []
    for r in range(PEER_TOPK):
        m = jnp.max(work, axis=0, keepdims=True)
        first = jnp.min(jnp.where(work == m, keys, 1e9), axis=0, keepdims=True)
        sel = keys == first
        step = jnp.where(sel, float(r), step)
        work = jnp.where(sel, -jnp.inf, work)
        vals.append(m)
    return vals, step


def _route_compute(s1, s2, exact):
    t = s1.shape[1]
    s = jnp.concatenate([s1, s2], axis=1)
    if exact:
        rows = lax.broadcasted_iota(jnp.int32, s.shape, 0).astype(F32)
        vals, rank = _take_top_in_order(s, rows)
        bad = jnp.zeros((), F32)
    else:
        vals = _top_values(s)
        rank = None
        taken = jnp.sum(jnp.where(s >= vals[-1], 1.0, 0.0), axis=0, keepdims=True)
        bad = jnp.max(jnp.abs(taken - PEER_TOPK))
    v1 = [v[:, :t] for v in vals]
    v2 = [v[:, t:] for v in vals]
    t2a = jnp.concatenate(v2[:8], axis=0)
    t2b = jnp.concatenate(v2[8:], axis=0)
    cand = jnp.concatenate([v1[0] + t2a, v1[0] + t2b] + [v1[r] + t2a for r in range(1, PEER_TOPK)], axis=0)
    if exact:
        i = lax.broadcasted_iota(jnp.int32, (CAND_ROWS, t), 0)
        flat = jnp.where(i < 16, i, 2 * i - (i & 7) - 16).astype(F32)
        _, step = _take_top_in_order(cand, flat)
        chosen = step < NOT_TOP
    else:
        chosen = cand >= _top_values(cand)[-1]
    cnt = jnp.where(chosen, 1.0, 0.0)
    if not exact:
        bad = jnp.maximum(bad, jnp.max(jnp.abs(jnp.sum(cnt, axis=0, keepdims=True) - PEER_TOPK)))
    top = v1[0] + v2[0]
    z = jnp.sum(jnp.where(chosen, jnp.exp(cand - top), 0.0), axis=0, keepdims=True)
    n = [jnp.sum(cnt[0:16], axis=0, keepdims=True)]
    n += [jnp.sum(cnt[8 * (r + 1):8 * (r + 2)], axis=0, keepdims=True) for r in range(1, PEER_TOPK)]
    n1 = jnp.zeros_like(s1)
    if exact:
        rank1, rank2 = rank[:, :t], rank[:, t:]
        for r in range(PEER_TOPK):
            n1 = jnp.where(rank1 == float(r), n[r], n1)
    else:
        rank2 = jnp.zeros_like(s2)
        for r in range(PEER_TOPK):
            n1 = jnp.where(s1 == v1[r], n[r], n1)
            rank2 = rank2 + jnp.where(s2 < v2[r], 1.0, 0.0)
    e2 = jnp.exp(s2 - v2[0])
    e1 = jnp.exp(s1 - v1[0]) / z
    return rank2, e2, n1, e1, bad


def _route_kernel(q_ref, keys_ref, k2_ref, e2_ref, n1_ref, e1_ref):
    q = q_ref[...].astype(BF)
    s1 = lax.dot_general(keys_ref[0], q[:, :N_KEYS], NT_DIMS, preferred_element_type=F32)
    s2 = lax.dot_general(keys_ref[1], q[:, N_KEYS:], NT_DIMS, preferred_element_type=F32)

    def emit(rank2, e2, n1, e1):
        k2_ref[...] = rank2.astype(k2_ref.dtype)
        e2_ref[...] = e2.astype(e2_ref.dtype)
        for c in range(s1.shape[1] // LANE):
            n1_ref[c] = n1[:, c * LANE:(c + 1) * LANE]
            e1_ref[c] = e1[:, c * LANE:(c + 1) * LANE]

    *fast, bad = _route_compute(s1, s2, exact=False)
    emit(*fast)

    @pl.when(bad > 0.0)
    def _():
        *slow, _ = _route_compute(s1, s2, exact=True)
        emit(*slow)


def _route(q, keys):
    tr = ROUTE_TR
    shp_bf = jax.ShapeDtypeStruct((PEER_HEADS, N_KEYS, NTOK), BF)
    ospec = pl.BlockSpec((None, N_KEYS, tr), lambda i, h: (h, 0, i))
    shp_a = jax.ShapeDtypeStruct((PEER_HEADS, NTOK // LANE, N_KEYS, LANE), F32)
    aspec = pl.BlockSpec((None, tr // LANE, N_KEYS, LANE), lambda i, h: (h, i, 0, 0))
    return pl.pallas_call(
        _route_kernel,
        grid=(NTOK // tr, PEER_HEADS),
        in_specs=[pl.BlockSpec((tr, PEER_DKEY), lambda i, h: (i, h)),
                  pl.BlockSpec((None, 2, N_KEYS, PEER_DKEY // 2), lambda i, h: (h, 0, 0, 0))],
        out_specs=[ospec, ospec, aspec, aspec],
        out_shape=[shp_bf, shp_bf, shp_a, shp_a],
        compiler_params=_params("arbitrary", "arbitrary"),
        name="peer_route",
    )(q, keys)


PEER_TM = 512
PEER_TE = 512
PEER_NJ = N_EXPERTS // PEER_TE
PEER_RC = 256
PEER_NC = D_MODEL // PEER_RC
PEER_PERIOD = PEER_NJ + 1 + PEER_NC
PEER_VMEM_LIMIT = 60 * 1024 * 1024


def _peer_step(h_ref, u_ref, act_w, k2_ref, e2_ref, n1_ref, e1_ref, act_r, wt_ref, jb):
    half_e, half_t = PEER_TE // 2, PEER_TM // 2
    for k in range(PEER_TE // N_KEYS):
        if act_w is not None:
            e0, t0 = half_e * (k // 2), half_t * (k % 2)
            act_w[e0:e0 + half_e, t0:t0 + half_t] = lax.dot_general(
                u_ref[e0:e0 + half_e, :], h_ref[t0:t0 + half_t, :], NT_DIMS, preferred_element_type=F32)
        if act_r is None:
            continue
        a = jb * (PEER_TE // N_KEYS) + k
        es = slice(k * N_KEYS, (k + 1) * N_KEYS)
        for lc in range(PEER_TM // LANE):
            ls = slice(lc * LANE, (lc + 1) * LANE)
            g = None
            for h in range(PEER_HEADS):
                n1 = jnp.broadcast_to(n1_ref[h, lc, pl.ds(a, 1), :], (N_KEYS, LANE)).astype(BF)
                e1 = jnp.broadcast_to(e1_ref[h, lc, pl.ds(a, 1), :], (N_KEYS, LANE)).astype(BF)
                t = jnp.where(k2_ref[h, :, ls] < n1, e2_ref[h, :, ls], jnp.zeros((), BF)) * e1
                g = t if g is None else g + t
            x = act_r[es, ls]
            erf = lax.erf((x * (2.0 ** -0.5)).astype(BF))
            wt_ref[jb, es, ls] = g * ((0.5 * x).astype(BF) * (1.0 + erf))


def _peer_kernel(h_ref, u_ref, v_ref, k2_ref, e2_ref, n1_ref, e1_ref, o_ref, act0, act1, wt_ref):
    r = pl.program_id(0) % PEER_PERIOD
    jb = jnp.maximum(r - 1, 0)
    gates = (k2_ref, e2_ref, n1_ref, e1_ref)

    @pl.when(r == 0)
    def _():
        _peer_step(h_ref, u_ref, act0, *gates, None, wt_ref, jb)

    @pl.when(jnp.logical_and(jnp.logical_and(r > 0, r < PEER_NJ), r % 2 == 1))
    def _():
        _peer_step(h_ref, u_ref, act1, *gates, act0, wt_ref, jb)

    @pl.when(jnp.logical_and(jnp.logical_and(r > 0, r < PEER_NJ), r % 2 == 0))
    def _():
        _peer_step(h_ref, u_ref, act0, *gates, act1, wt_ref, jb)

    @pl.when(r == PEER_NJ)
    def _():
        _peer_step(h_ref, u_ref, None, *gates, act1 if PEER_NJ % 2 == 0 else act0, wt_ref, jb)

    @pl.when(r > PEER_NJ)
    def _():
        wt = wt_ref[...].reshape(N_EXPERTS, PEER_TM)
        o_ref[...] = jnp.dot(v_ref[...], wt, preferred_element_type=F32).astype(o_ref.dtype)


def _peer(h2, u, vt, k2, e2, n1, e1):
    tm, te, nj, rc, period = PEER_TM, PEER_TE, PEER_NJ, PEER_RC, PEER_PERIOD

    def tile(s):
        return s // period

    def expert_step(s):
        return jnp.minimum(s % period, nj - 1)

    def row_step(s):
        return jnp.maximum(s % period - (nj + 1), 0)

    rspec = pl.BlockSpec((PEER_HEADS, N_KEYS, tm), lambda s: (0, 0, tile(s)), pipeline_mode=pl.Buffered(1))
    aspec = pl.BlockSpec((PEER_HEADS, tm // LANE, N_KEYS, LANE), lambda s: (0, tile(s), 0, 0),
                         pipeline_mode=pl.Buffered(1))
    return pl.pallas_call(
        _peer_kernel,
        grid=((NTOK // tm) * period,),
        in_specs=[pl.BlockSpec((tm, D_MODEL), lambda s: (tile(s), 0), pipeline_mode=pl.Buffered(1)),
                  pl.BlockSpec((te, D_MODEL), lambda s: (expert_step(s), 0)),
                  pl.BlockSpec((rc, N_EXPERTS), lambda s: (row_step(s), 0)),
                  rspec, rspec, aspec, aspec],
        out_specs=pl.BlockSpec((rc, tm), lambda s: (row_step(s), tile(s))),
        out_shape=jax.ShapeDtypeStruct((D_MODEL, NTOK), BF),
        scratch_shapes=[pltpu.VMEM((te, tm), F32), pltpu.VMEM((te, tm), F32), pltpu.VMEM((nj, te, tm), BF)],
        compiler_params=pltpu.CompilerParams(dimension_semantics=("arbitrary",), vmem_limit_bytes=PEER_VMEM_LIMIT),
        name="peer_dense",
    )(h2, u, vt, k2, e2, n1, e1)


def _final_kernel(x_ref, pt_ref, gate_ref, g_ref, o_ref):
    y = x_ref[...] + gate_ref[...] * pt_ref[...].astype(F32).T
    ms = jnp.mean(y * y, axis=-1, keepdims=True)
    o_ref[...] = y * lax.rsqrt(ms + EPS) * g_ref[...]


def _final(x1, peer_t, modr, final_g):
    tr = 256
    return pl.pallas_call(
        _final_kernel,
        grid=(NTOK // tr,),
        in_specs=[pl.BlockSpec((tr, D_MODEL), lambda i: (i, 0)),
                  pl.BlockSpec((D_MODEL, tr), lambda i: (0, i)),
                  pl.BlockSpec((None, 1, D_MODEL), lambda i: ((i * tr) // SEQ * N_MOD + 5, 0, 0)),
                  pl.BlockSpec((1, D_MODEL), lambda i: (0, 0))],
        out_specs=pl.BlockSpec((tr, D_MODEL), lambda i: (i, 0)),
        out_shape=jax.ShapeDtypeStruct((NTOK, D_MODEL), F32),
        compiler_params=_params("arbitrary"),
        name="final_norm",
    )(x1, peer_t, modr, final_g)


def kernel(x, c, w_ada, b_ada, norm1_g, norm2_g, w_in, attn_sinks, rel_bias, gla_w_gk2, gla_b_gk2,
           gla_norm_g, w_out, peer_w_q, peer_keys, peer_u, peer_v, final_g):
    assert x.shape == (BATCH, SEQ, D_MODEL) and w_ada.shape[0] == 1
    x2 = x.reshape(NTOK, D_MODEL)
    c8 = jnp.pad(c, ((0, 8 - BATCH), (0, 0)))
    mod = _ada(c8, w_ada[0], b_ada[0][None, :])
    modr = mod[:BATCH].reshape(BATCH * N_MOD, 1, D_MODEL)

    wt = jnp.swapaxes(w_in, 1, 2)[0]

    h1 = _norm_mod(x2, norm1_g[0][None, :], modr, 1, 0)
    proj = _matmul_t(h1, wt, 0, OFF_GLOW, BF, 1024, 512, "in_proj")
    gout = _matmul_t(h1, wt, OFF_GOUT, GLA_V, BF, 1024, 512, "gout_proj")
    glow = _matmul_t(h1, wt, OFF_GLOW, GLA_RANK, F32, 1024, GLA_RANK, "glow_proj")

    bias_tab = _bias_table(rel_bias, attn_sinks[0])
    attn = _attention(proj, bias_tab)
    gla = _gla(proj, gout, glow, gla_w_gk2[0], gla_b_gk2[0][None, :], gla_norm_g[0][None, :])
    x1 = _outproj(attn, gla, w_out[0], x2, modr)

    h2 = _norm_mod(x1, norm2_g[0][None, :], modr, 4, 3)
    q = _matmul(h2, peer_w_q[0], PEER_HEADS * PEER_DKEY, F32, 1024, 512, "peer_q")
    k2, e2, n1, e1 = _route(q, peer_keys[0].astype(BF))
    peer_t = _peer(h2, peer_u[0].astype(BF), peer_v[0].T.astype(BF), k2, e2, n1, e1)
    out = _final(x1, peer_t, modr, final_g[None, :])
    return out.reshape(BATCH, SEQ, D_MODEL)
```
